```python
import jax, jax.numpy as jnp
from jax import lax
import numpy as np

D_MODEL = 2048
BATCH = 4
SEQ = 2048
DEPTH = 2

GRID_W = 64
CTX_LEN = 256

N_HEADS = 16
N_KV_HEADS = 4
HEAD_DIM = D_MODEL // N_HEADS
Q_BLOCK = 128
ROPE_THETA = 10000.0
CHUNK = 128
GMLP_WIDTH = D_MODEL // 2
GMLP_GROUPS = 8
GMLP_GROUP_DIM = GMLP_WIDTH // GMLP_GROUPS
CONV_WIDTH = D_MODEL // 2
CONV_K = 3
D_FF = ((8 * D_MODEL // 3 + 127) // 128) * 128
FFN_CONV_K = 3
EPS = 1e-6

Q_DIM = N_HEADS * HEAD_DIM
KV_DIM = N_KV_HEADS * HEAD_DIM
N_BRANCH = 3
SPLITS = (Q_DIM, KV_DIM, KV_DIM, 2 * GMLP_WIDTH, CONV_WIDTH, CONV_WIDTH, CONV_WIDTH, N_BRANCH * D_MODEL)
IN_DIM = sum(SPLITS)

kernel_name = "hybrid_gmlp_shortconv_gqa_dit_block"


def rms_norm(x, g):
    xf = x.astype(jnp.float32)
    y = xf * lax.rsqrt(jnp.mean(xf * xf, axis=-1, keepdims=True) + EPS)
    return (y * g.astype(jnp.float32)).astype(x.dtype)


def modulate(h, shift, scale):
    return h * (1 + scale) + shift


def dwconv_centred(x, w):
    K = w.shape[0]
    pad = K // 2
    L = x.shape[1]
    xp = jnp.pad(x, ((0, 0), (pad, pad), (0, 0)))
    out = xp[:, 0:L, :] * w[0]
    for k in range(1, K):
        out = out + xp[:, k:k + L, :] * w[k]
    return out


def split_proj(z):
    points = np.cumsum(SPLITS)[:-1].tolist()
    return jnp.split(z, points, axis=-1)


def axial_rope_tables(L):
    rows = L // GRID_W
    row = jnp.repeat(jnp.arange(rows, dtype=jnp.float32), GRID_W)
    col = jnp.tile(jnp.arange(GRID_W, dtype=jnp.float32), rows)
    n_freq = HEAD_DIM // 4
    inv_freq = ROPE_THETA ** (-jnp.arange(n_freq, dtype=jnp.float32) / n_freq)
    ang = jnp.stack([row, col], axis=-1)[..., None] * inv_freq
    return jnp.cos(ang), jnp.sin(ang)


def apply_axial_rope(x, cos, sin):
    B, L, H, _ = x.shape
    xr = x.astype(jnp.float32).reshape(B, L, H, 2, 2, HEAD_DIM // 4)
    x1, x2 = xr[..., 0, :], xr[..., 1, :]
    cs, sn = cos[None, :, None], sin[None, :, None]
    out = jnp.stack([x1 * cs - x2 * sn, x2 * cs + x1 * sn], axis=-2)
    return out.reshape(B, L, H, HEAD_DIM).astype(x.dtype)


def qkv_heads(q, k, v, g_q, g_k, rope=None):
    B, L = q.shape[:2]
    q = rms_norm(q.reshape(B, L, N_HEADS, HEAD_DIM), g_q)
    k = rms_norm(k.reshape(B, L, N_KV_HEADS, HEAD_DIM), g_k)
    v = v.reshape(B, L, N_KV_HEADS, HEAD_DIM)
    if rope is not None:
        q = apply_axial_rope(q, *rope)
        k = apply_axial_rope(k, *rope)
    return q, k, v


def attend(qg, keys, vals):
    s = jnp.einsum('bqkgd,bskd->bkgqs', qg, keys).astype(jnp.float32) * (HEAD_DIM ** -0.5)
    p = jax.nn.softmax(s, axis=-1).astype(vals.dtype)
    return jnp.einsum('bkgqs,bskd->bqkgd', p, vals)


def context_attention(q, k, v):
    B, Lc = q.shape[:2]
    qg = q.reshape(B, Lc, N_KV_HEADS, N_HEADS // N_KV_HEADS, HEAD_DIM)
    return attend(qg, k, v).reshape(B, Lc, Q_DIM)


def latent_attention(q, k, v, k_ctx, v_ctx):
    B, L = q.shape[:2]
    keys = jnp.concatenate([k, k_ctx], axis=1)
    vals = jnp.concatenate([v, v_ctx], axis=1)
    qb = q.reshape(B, L // Q_BLOCK, Q_BLOCK, N_KV_HEADS, N_HEADS // N_KV_HEADS, HEAD_DIM)
    qb = qb.transpose(1, 0, 2, 3, 4, 5)
    o = lax.map(lambda qi: attend(qi, keys, vals), qb)
    return o.transpose(1, 0, 2, 3, 4, 5).reshape(B, L, Q_DIM)


def chunk_gmlp(z, w_s, b_s, g_v):
    z = jax.nn.gelu(z)
    u, v = jnp.split(z, 2, axis=-1)
    v = rms_norm(v, g_v)
    B, L, _ = v.shape
    vc = v.reshape(B, L // CHUNK, CHUNK, GMLP_GROUPS, GMLP_GROUP_DIM)
    mixed = jnp.einsum('gpq,bnqgc->bnpgc', w_s, vc) + b_s.T[None, None, :, :, None]
    return u * mixed.reshape(B, L, GMLP_WIDTH)


def short_conv(b_gate, c_gate, h, w_conv):
    return b_gate * dwconv_centred(c_gate * h, w_conv)


def merge_branches(parts, attn, w_s, b_s, g_v, w_sconv, w_pa, w_pb, w_pc, w_o):
    _, _, _, z_a, b_gate, c_gate, h_b, gates = parts
    y_a = chunk_gmlp(z_a, w_s, b_s, g_v)
    y_b = short_conv(b_gate, c_gate, h_b, w_sconv)
    g_a, g_b, g_c = jnp.split(jax.nn.sigmoid(gates), N_BRANCH, axis=-1)
    m = g_a * (y_a @ w_pa) + g_b * (y_b @ w_pb) + g_c * (attn @ w_pc)
    return m @ w_o


def conv_ffn(h, w_up, w_conv, w_down):
    u = dwconv_centred(h @ w_up, w_conv)
    a, b = jnp.split(u, 2, axis=-1)
    return (jax.nn.silu(a) * b) @ w_down


def setup_inputs(seed: int = 0) -> dict:
    key = jax.random.key(seed)
    ks = jax.random.split(key, 32)
    f32 = jnp.float32

    def nrm(k, shape, scale):
        return jax.random.normal(k, shape, f32) * scale

    def gain(k, shape):
        return 1.0 + 0.02 * jax.random.normal(k, shape, f32)

    D = D_MODEL
    return {
        "x": nrm(ks[0], (BATCH, SEQ, D), 1.0),
        "c": nrm(ks[1], (BATCH, D), 1.0),
        "ctx": nrm(ks[2], (BATCH, CTX_LEN, D), 1.0),
        "c_ctx": nrm(ks[3], (D,), 1.0),
        "w_ada": nrm(ks[4], (DEPTH, D, 6 * D), 0.5 * D ** -0.5),
        "b_ada": nrm(ks[5], (DEPTH, 6 * D), 0.01),
        "g_mix": gain(ks[6], (DEPTH, D)),
        "w_in": nrm(ks[7], (DEPTH, D, IN_DIM), D ** -0.5),
        "g_q": gain(ks[8], (DEPTH, HEAD_DIM)),
        "g_k": gain(ks[9], (DEPTH, HEAD_DIM)),
        "w_gmlp": nrm(ks[10], (DEPTH, GMLP_GROUPS, CHUNK, CHUNK), CHUNK ** -0.5),
        "b_gmlp": gain(ks[11], (DEPTH, GMLP_GROUPS, CHUNK)),
        "g_gmlp_v": gain(ks[12], (DEPTH, GMLP_WIDTH)),
        "w_sconv": nrm(ks[13], (DEPTH, CONV_K, CONV_WIDTH), CONV_K ** -0.5),
        "w_pa": nrm(ks[14], (DEPTH, GMLP_WIDTH, D), GMLP_WIDTH ** -0.5),
        "w_pb": nrm(ks[15], (DEPTH, CONV_WIDTH, D), CONV_WIDTH ** -0.5),
        "w_pc": nrm(ks[16], (DEPTH, Q_DIM, D), Q_DIM ** -0.5),
        "w_o": nrm(ks[17], (DEPTH, D, D), D ** -0.5),
        "g_ffn": gain(ks[18], (DEPTH, D)),
        "w_up": nrm(ks[19], (DEPTH, D, 2 * D_FF), D ** -0.5),
        "w_ffn_conv": nrm(ks[20], (DEPTH, FFN_CONV_K, 2 * D_FF), FFN_CONV_K ** -0.5),
        "w_down": nrm(ks[21], (DEPTH, D_FF, D), D_FF ** -0.5),
        "g_final": gain(ks[22], (D,)),
    }


def reference(x, c, ctx, c_ctx, w_ada, b_ada, g_mix, w_in, g_q, g_k, w_gmlp, b_gmlp, g_gmlp_v,
              w_sconv, w_pa, w_pb, w_pc, w_o, g_ffn, w_up, w_ffn_conv, w_down, g_final):
    L = x.shape[1]
    rope = axial_rope_tables(L)
    xc = ctx
    for l in range(DEPTH):
        last = l == DEPTH - 1
        mod_x = jax.nn.silu(c) @ w_ada[l] + b_ada[l]
        mod_c = jax.nn.silu(c_ctx) @ w_ada[l] + b_ada[l]
        sh1, sc1, gt1, sh2, sc2, gt2 = jnp.split(mod_x[:, None, :], 6, axis=-1)
        csh1, csc1, cgt1, csh2, csc2, cgt2 = jnp.split(mod_c, 6, axis=-1)
        mix_args = (w_gmlp[l], b_gmlp[l], g_gmlp_v[l], w_sconv[l], w_pa[l], w_pb[l], w_pc[l], w_o[l])

        hc = modulate(rms_norm(xc, g_mix[l]), csh1, csc1)
        hx = modulate(rms_norm(x, g_mix[l]), sh1, sc1)
        pc = split_proj(hc @ w_in[l])
        px = split_proj(hx @ w_in[l])
        qc, kc, vc = qkv_heads(pc[0], pc[1], pc[2], g_q[l], g_k[l])
        qx, kx, vx = qkv_heads(px[0], px[1], px[2], g_q[l], g_k[l], rope)
        attn_x = latent_attention(qx, kx, vx, kc, vc)
        x = x + gt1 * merge_branches(px, attn_x, *mix_args)

        fx = modulate(rms_norm(x, g_ffn[l]), sh2, sc2)
        x = x + gt2 * conv_ffn(fx, w_up[l], w_ffn_conv[l], w_down[l])

        if not last:
            attn_c = context_attention(qc, kc, vc)
            xc = xc + cgt1 * merge_branches(pc, attn_c, *mix_args)
            fc = modulate(rms_norm(xc, g_ffn[l]), csh2, csc2)
            xc = xc + cgt2 * conv_ffn(fc, w_up[l], w_ffn_conv[l], w_down[l])

    return rms_norm(x, g_final)
```

```python
import functools
import math

import jax
import jax.numpy as jnp
from jax.experimental import pallas as pl
from jax.experimental.pallas import tpu as pltpu

F32 = jnp.float32
BF16 = jnp.bfloat16

D = 2048
BATCH = 4
SEQ = 2048
CTX = 256
DEPTH = 2
GRID_W = 64
N_HEADS = 16
N_KV = 4
HD = 128
GROUP = N_HEADS // N_KV
CHUNK = 128
GW = D // 2
GG = 8
CW = D // 2
D_FF = 5504
D_FF_PAD = 5632
EPS = 1e-6
ROPE_THETA = 10000.0
KV_DIM = N_KV * HD
IN_DIM = D + 2 * KV_DIM + 2 * GW + 3 * CW + 3 * D
OFF_Q = 0
OFF_K = D
OFF_V = D + KV_DIM
OFF_U = D + 2 * KV_DIM
OFF_VG = OFF_U + GW
OFF_B = OFF_VG + GW
OFF_C = OFF_B + CW
OFF_H = OFF_C + CW
OFF_G = OFF_H + CW

VMEM_PHYS_V7X = 64 * 1024 * 1024
VMEM_CAP = VMEM_PHYS_V7X - 8 * 1024 * 1024


def _cp(sem, vmem_mb):
    return pltpu.CompilerParams(
        dimension_semantics=sem,
        vmem_limit_bytes=min(int(vmem_mb * 1024 * 1024), VMEM_CAP))


def _rms(xf, g):
    ms = jnp.mean(xf * xf, axis=-1, keepdims=True)
    return xf * jax.lax.rsqrt(ms + EPS) * g


def _sigmoid(x):
    return 0.5 * jnp.tanh(0.5 * x) + 0.5


def _gelu_tanh(x):
    c = math.sqrt(2.0 / math.pi)
    return 0.5 * x * (1.0 + jnp.tanh(c * (x + 0.044715 * (x * x * x))))


def _mods_kernel(c_ref, w_ref, b_ref, o_ref):
    c = c_ref[...]
    s = (c * _sigmoid(c)).astype(BF16)
    w = w_ref[...].astype(BF16)
    o_ref[...] = jnp.dot(s, w, preferred_element_type=F32) + b_ref[...]


def _mods(cvec, w_ada, b_ada):
    tn = 1024
    nj = 6 * D // tn
    return pl.pallas_call(
        _mods_kernel,
        out_shape=jax.ShapeDtypeStruct((DEPTH, 8, 6 * D), F32),
        grid=(DEPTH, nj),
        in_specs=[
            pl.BlockSpec((8, D), lambda l, j: (0, 0)),
            pl.BlockSpec((None, D, tn), lambda l, j: (l, 0, j)),
            pl.BlockSpec((None, 1, tn), lambda l, j: (l, 0, j)),
        ],
        out_specs=pl.BlockSpec((None, 8, tn), lambda l, j: (l, 0, j)),
        compiler_params=_cp(("arbitrary", "arbitrary"), 40),
        name="adaln_mods",
    )(cvec, w_ada, b_ada.reshape(DEPTH, 1, 6 * D))


def _mod_spec(layer, k, tiles_per_batch, ctx, ngrid):
    if ngrid == 1:
        if ctx:
            return pl.BlockSpec((None, None, None, 1, D), lambda i: (layer, BATCH, k, 0, 0))
        return pl.BlockSpec((None, None, None, 1, D),
                            lambda i: (layer, i // tiles_per_batch, k, 0, 0))
    if ctx:
        return pl.BlockSpec((None, None, None, 1, D), lambda i, j: (layer, BATCH, k, 0, 0))
    return pl.BlockSpec((None, None, None, 1, D),
                        lambda i, j: (layer, i // tiles_per_batch, k, 0, 0))


def _normmod_kernel(x_ref, g_ref, sh_ref, sc_ref, o_ref):
    y = _rms(x_ref[...], g_ref[...])
    o_ref[...] = (y * (1.0 + sc_ref[...]) + sh_ref[...]).astype(o_ref.dtype)


def _normmod(x, g, mods5, layer, ctx):
    t = x.shape[0]
    tm = 512
    tpb = (SEQ // tm) if not ctx else 1
    return pl.pallas_call(
        _normmod_kernel,
        out_shape=jax.ShapeDtypeStruct((t, D), BF16),
        grid=(t // tm,),
        in_specs=[
            pl.BlockSpec((tm, D), lambda i: (i, 0)),
            pl.BlockSpec((1, D), lambda i: (0, 0)),
            _mod_spec(layer, 0, tpb, ctx, 1),
            _mod_spec(layer, 1, tpb, ctx, 1),
        ],
        out_specs=pl.BlockSpec((tm, D), lambda i: (i, 0)),
        compiler_params=_cp(("arbitrary",), 32),
        name="norm_mod",
    )(x, g.reshape(1, D), mods5, mods5)


TN_IN = 1024


def _inproj_kernel(h_ref, w_ref, o_ref, *, col_off):
    j = pl.program_id(1) + col_off
    acc = jnp.dot(h_ref[...], w_ref[...], preferred_element_type=F32)
    j_gelu0 = OFF_U // TN_IN
    j_gelu1 = OFF_B // TN_IN
    j_gate0 = OFF_G // TN_IN
    is_gelu = jnp.logical_and(j >= j_gelu0, j < j_gelu1)
    is_gate = j >= j_gate0

    @pl.when(is_gelu)
    def _():
        o_ref[...] = _gelu_tanh(acc).astype(o_ref.dtype)

    @pl.when(is_gate)
    def _():
        o_ref[...] = _sigmoid(acc).astype(o_ref.dtype)

    @pl.when(jnp.logical_not(jnp.logical_or(is_gelu, is_gate)))
    def _():
        o_ref[...] = acc.astype(o_ref.dtype)


def _inproj(h, w, col_off, ncols):
    t = h.shape[0]
    tm = 1024
    nj = ncols // TN_IN
    return pl.pallas_call(
        functools.partial(_inproj_kernel, col_off=col_off),
        out_shape=jax.ShapeDtypeStruct((t, ncols), BF16),
        grid=(t // tm, nj),
        in_specs=[
            pl.BlockSpec((tm, D), lambda i, j: (i, 0)),
            pl.BlockSpec((D, TN_IN), lambda i, j: (0, j + col_off)),
        ],
        out_specs=pl.BlockSpec((tm, TN_IN), lambda i, j: (i, j)),
        compiler_params=_cp(("arbitrary", "arbitrary"), 48),
        name="in_proj",
    )(h, w)


def _rope_tables():
    rows = SEQ // GRID_W
    row = jnp.repeat(jnp.arange(rows, dtype=F32), GRID_W)
    col = jnp.tile(jnp.arange(GRID_W, dtype=F32), rows)
    n_freq = HD // 4
    inv_freq = ROPE_THETA ** (-jnp.arange(n_freq, dtype=F32) / n_freq)
    ar = row[:, None] * inv_freq
    ac = col[:, None] * inv_freq
    cos = jnp.concatenate([jnp.cos(ar), jnp.cos(ar), jnp.cos(ac), jnp.cos(ac)], axis=-1)
    sin = jnp.concatenate([-jnp.sin(ar), jnp.sin(ar), -jnp.sin(ac), jnp.sin(ac)], axis=-1)
    return cos, sin


def _headnorm_kernel(x_ref, g_ref, *rest, n_heads, scale, rope):
    if rope:
        cos_ref, sin_ref, o_ref = rest
        cos = cos_ref[...]
        sin = sin_ref[...]
        lane = jax.lax.broadcasted_iota(jnp.int32, cos.shape, 1)
        first_half = (lane % (HD // 2)) < (HD // 4)
    else:
        (o_ref,) = rest
    g = g_ref[...] * scale
    for h in range(n_heads):
        x = x_ref[:, h * HD:(h + 1) * HD].astype(F32)
        y = _rms(x, g)
        if rope:
            partner = jnp.where(first_half,
                                pltpu.roll(y, HD - HD // 4, axis=1),
                                pltpu.roll(y, HD // 4, axis=1))
            y = y * cos + partner * sin
        o_ref[:, h * HD:(h + 1) * HD] = y.astype(o_ref.dtype)


def _headnorm(src, col_blk, n_heads, g, scale, rope_tabs):
    t = src.shape[0]
    tm = 512
    w = n_heads * HD
    rope = rope_tabs is not None
    in_specs = [
        pl.BlockSpec((tm, w), lambda i: (i, col_blk)),
        pl.BlockSpec((1, HD), lambda i: (0, 0)),
    ]
    args = [src, g.reshape(1, HD)]
    if rope:
        nt = SEQ // tm
        in_specs += [pl.BlockSpec((tm, HD), lambda i: (i % nt, 0))] * 2
        args += list(rope_tabs)
    return pl.pallas_call(
        functools.partial(_headnorm_kernel, n_heads=n_heads, scale=scale, rope=rope),
        out_shape=jax.ShapeDtypeStruct((t, w), BF16),
        grid=(t // tm,),
        in_specs=in_specs,
        out_specs=pl.BlockSpec((tm, w), lambda i: (i, 0)),
        compiler_params=_cp(("arbitrary",), 32),
        name="head_norm",
    )(*args)


def _attn_kernel(q_ref, k_ref, v_ref, o_ref):
    k = k_ref[...]
    v = v_ref[...]
    for g in range(GROUP):
        q = q_ref[:, g * HD:(g + 1) * HD]
        s = jax.lax.dot_general(q, k, (((1,), (1,)), ((), ())),
                                preferred_element_type=F32)
        m = jnp.max(s, axis=-1, keepdims=True)
        p = jnp.exp2(s - m)
        l = jnp.sum(p, axis=-1, keepdims=True)
        o = jnp.dot(p.astype(BF16), v, preferred_element_type=F32)
        o_ref[:, g * HD:(g + 1) * HD] = (o / l).astype(o_ref.dtype)


def _attention(q, k, v, lq):
    s_len = k.shape[1]
    tq = 256
    nq = lq // tq
    return pl.pallas_call(
        _attn_kernel,
        out_shape=jax.ShapeDtypeStruct((BATCH * lq, D), BF16),
        grid=(BATCH, N_KV, nq),
        in_specs=[
            pl.BlockSpec((tq, GROUP * HD), lambda b, h, i: (b * nq + i, h)),
            pl.BlockSpec((None, s_len, HD), lambda b, h, i: (b, 0, h)),
            pl.BlockSpec((None, s_len, HD), lambda b, h, i: (b, 0, h)),
        ],
        out_specs=pl.BlockSpec((tq, GROUP * HD), lambda b, h, i: (b * nq + i, h)),
        compiler_params=_cp(("arbitrary", "arbitrary", "arbitrary"), 40),
        name="attention",
    )(q, k, v)


def _gmlp_kernel(u_ref, v_ref, gv_ref, ws_ref, bs_ref, o_ref, *, tm):
    v = _rms(v_ref[...].astype(F32), gv_ref[...]).astype(BF16)
    for n in range(tm // CHUNK):
        r0 = n * CHUNK
        for g in range(GG):
            c0 = g * CHUNK
            mixed = jnp.dot(ws_ref[g], v[r0:r0 + CHUNK, c0:c0 + CHUNK],
                            preferred_element_type=F32) + bs_ref[g]
            u = u_ref[r0:r0 + CHUNK, c0:c0 + CHUNK].astype(F32)
            o_ref[r0:r0 + CHUNK, c0:c0 + CHUNK] = (u * mixed).astype(o_ref.dtype)


def _gmlp(px, gv, ws, bs):
    t = px.shape[0]
    tm = 512
    return pl.pallas_call(
        functools.partial(_gmlp_kernel, tm=tm),
        out_shape=jax.ShapeDtypeStruct((t, GW), BF16),
        grid=(t // tm,),
        in_specs=[
            pl.BlockSpec((tm, GW), lambda i: (i, OFF_U // GW)),
            pl.BlockSpec((tm, GW), lambda i: (i, OFF_VG // GW)),
            pl.BlockSpec((1, GW), lambda i: (0, 0)),
            pl.BlockSpec((GG, CHUNK, CHUNK), lambda i: (0, 0, 0)),
            pl.BlockSpec((GG, CHUNK, CHUNK), lambda i: (0, 0, 0)),
        ],
        out_specs=pl.BlockSpec((tm, GW), lambda i: (i, 0)),
        compiler_params=_cp(("arbitrary",), 32),
        name="gmlp",
    )(px, px, gv.reshape(1, GW), ws, bs)


def _dwconv3(x, w_ref, seq_len):
    rows = x.shape[0]
    pos = jax.lax.broadcasted_iota(jnp.int32, x.shape, 0) % seq_len
    prev = jnp.where(pos == 0, 0.0, pltpu.roll(x, 1, axis=0))
    nxt = jnp.where(pos == seq_len - 1, 0.0, pltpu.roll(x, rows - 1, axis=0))
    return prev * w_ref[0:1, :] + x * w_ref[1:2, :] + nxt * w_ref[2:3, :]


def _sconv_kernel(b_ref, c_ref, h_ref, w_ref, o_ref, *, seq_len):
    t = c_ref[...].astype(F32) * h_ref[...].astype(F32)
    y = b_ref[...].astype(F32) * _dwconv3(t, w_ref, seq_len)
    o_ref[...] = y.astype(o_ref.dtype)


def _sconv(px, w, seq_len, rows):
    t = px.shape[0]
    tc = 256
    return pl.pallas_call(
        functools.partial(_sconv_kernel, seq_len=seq_len),
        out_shape=jax.ShapeDtypeStruct((t, CW), BF16),
        grid=(t // rows, CW // tc),
        in_specs=[
            pl.BlockSpec((rows, tc), lambda i, j: (i, OFF_B // tc + j)),
            pl.BlockSpec((rows, tc), lambda i, j: (i, OFF_C // tc + j)),
            pl.BlockSpec((rows, tc), lambda i, j: (i, OFF_H // tc + j)),
            pl.BlockSpec((3, tc), lambda i, j: (0, j)),
        ],
        out_specs=pl.BlockSpec((rows, tc), lambda i, j: (i, j)),
        compiler_params=_cp(("arbitrary", "arbitrary"), 40),
        name="short_conv",
    )(px, px, px, w)


def _merge_kernel(ya_ref, yb_ref, at_ref, wa_ref, wb_ref, wc_ref,
                  ga_ref, gb_ref, gc_ref, o_ref):
    a = jnp.dot(ya_ref[...], wa_ref[...], preferred_element_type=F32)
    m = ga_ref[...].astype(F32) * a
    b = jnp.dot(yb_ref[...], wb_ref[...], preferred_element_type=F32)
    m = m + gb_ref[...].astype(F32) * b
    c = jnp.dot(at_ref[...], wc_ref[...], preferred_element_type=F32)
    m = m + gc_ref[...].astype(F32) * c
    o_ref[...] = m.astype(o_ref.dtype)


def _merge(ya, yb, attn, px, w_pa, w_pb, w_pc):
    t = ya.shape[0]
    tm = 1024
    tn = 512
    gb0 = OFF_G // tn
    return pl.pallas_call(
        _merge_kernel,
        out_shape=jax.ShapeDtypeStruct((t, D), BF16),
        grid=(t // tm, D // tn),
        in_specs=[
            pl.BlockSpec((tm, GW), lambda i, j: (i, 0)),
            pl.BlockSpec((tm, CW), lambda i, j: (i, 0)),
            pl.BlockSpec((tm, D), lambda i, j: (i, 0)),
            pl.BlockSpec((GW, tn), lambda i, j: (0, j)),
            pl.BlockSpec((CW, tn), lambda i, j: (0, j)),
            pl.BlockSpec((D, tn), lambda i, j: (0, j)),
            pl.BlockSpec((tm, tn), lambda i, j: (i, gb0 + j)),
            pl.BlockSpec((tm, tn), lambda i, j: (i, gb0 + D // tn + j)),
            pl.BlockSpec((tm, tn), lambda i, j: (i, gb0 + 2 * (D // tn) + j)),
        ],
        out_specs=pl.BlockSpec((tm, tn), lambda i, j: (i, j)),
        compiler_params=_cp(("arbitrary", "arbitrary"), 48),
        name="merge",
    )(ya, yb, attn, w_pa, w_pb, w_pc, px, px, px)


def _outproj_kernel(m_ref, w_ref, x_ref, gt_ref, g_ref, sh_ref, sc_ref, x1_ref, h_ref):
    y = jnp.dot(m_ref[...], w_ref[...], preferred_element_type=F32)
    x1 = x_ref[...] + gt_ref[...] * y
    x1_ref[...] = x1
    h_ref[...] = (_rms(x1, g_ref[...]) * (1.0 + sc_ref[...]) + sh_ref[...]).astype(h_ref.dtype)


def _outproj(m, w_o, x, g_ffn, mods5, layer, ctx):
    t = m.shape[0]
    tm = 512
    tpb = (SEQ // tm) if not ctx else 1
    return pl.pallas_call(
        _outproj_kernel,
        out_shape=(jax.ShapeDtypeStruct((t, D), F32), jax.ShapeDtypeStruct((t, D), BF16)),
        grid=(t // tm,),
        in_specs=[
            pl.BlockSpec((tm, D), lambda i: (i, 0)),
            pl.BlockSpec((D, D), lambda i: (0, 0)),
            pl.BlockSpec((tm, D), lambda i: (i, 0)),
            _mod_spec(layer, 2, tpb, ctx, 1),
            pl.BlockSpec((1, D), lambda i: (0, 0)),
            _mod_spec(layer, 3, tpb, ctx, 1),
            _mod_spec(layer, 4, tpb, ctx, 1),
        ],
        out_specs=(pl.BlockSpec((tm, D), lambda i: (i, 0)),
                   pl.BlockSpec((tm, D), lambda i: (i, 0))),
        compiler_params=_cp(("arbitrary",), 52),
        name="out_proj",
    )(m, w_o, x, mods5, g_ffn.reshape(1, D), mods5, mods5)


def _ffn_up_kernel(h_ref, wa_ref, wb_ref, ca_ref, cb_ref, o_ref, *, seq_len):
    h = h_ref[...]
    a = _dwconv3(jnp.dot(h, wa_ref[...], preferred_element_type=F32), ca_ref, seq_len)
    b = _dwconv3(jnp.dot(h, wb_ref[...], preferred_element_type=F32), cb_ref, seq_len)
    o_ref[...] = (a * _sigmoid(a) * b).astype(o_ref.dtype)


def _ffn_up(h, wa, wb, ca, cb, seq_len, rows):
    t = h.shape[0]
    tn = 256
    return pl.pallas_call(
        functools.partial(_ffn_up_kernel, seq_len=seq_len),
        out_shape=jax.ShapeDtypeStruct((t, D_FF_PAD), BF16),
        grid=(t // rows, D_FF_PAD // tn),
        in_specs=[
            pl.BlockSpec((rows, D), lambda i, j: (i, 0)),
            pl.BlockSpec((D, tn), lambda i, j: (0, j)),
            pl.BlockSpec((D, tn), lambda i, j: (0, j)),
            pl.BlockSpec((3, tn), lambda i, j: (0, j)),
            pl.BlockSpec((3, tn), lambda i, j: (0, j)),
        ],
        out_specs=pl.BlockSpec((rows, tn), lambda i, j: (i, j)),
        compiler_params=_cp(("arbitrary", "arbitrary"), 52),
        name="ffn_up",
    )(h, wa, wb, ca, cb)


def _ffn_down_kernel(g_ref, w_ref, x_ref, gt_ref, gn_ref, *rest, nk, mode):
    if mode == "mod":
        sh_ref, sc_ref, x2_ref, h_ref, acc_ref = rest
    else:
        o_ref, acc_ref = rest
    k = pl.program_id(1)

    @pl.when(k == 0)
    def _():
        acc_ref[...] = jnp.zeros_like(acc_ref)

    acc_ref[...] += jnp.dot(g_ref[...], w_ref[...], preferred_element_type=F32)

    @pl.when(k == nk - 1)
    def _():
        x2 = x_ref[...] + gt_ref[...] * acc_ref[...]
        y = _rms(x2, gn_ref[...])
        if mode == "mod":
            x2_ref[...] = x2
            h_ref[...] = (y * (1.0 + sc_ref[...]) + sh_ref[...]).astype(h_ref.dtype)
        else:
            o_ref[...] = y


def _ffn_down(g, w_down, x, g_next, mods5, layer, ctx, mode):
    t = g.shape[0]
    tm = 512
    tk = 512
    nk = D_FF_PAD // tk
    tpb = (SEQ // tm) if not ctx else 1
    in_specs = [
        pl.BlockSpec((tm, tk), lambda i, k: (i, k)),
        pl.BlockSpec((tk, D), lambda i, k: (k, 0)),
        pl.BlockSpec((tm, D), lambda i, k: (i, 0)),
        _mod_spec(layer, 5, tpb, ctx, 2),
        pl.BlockSpec((1, D), lambda i, k: (0, 0)),
    ]
    args = [g, w_down, x, mods5, g_next.reshape(1, D)]
    row_spec = pl.BlockSpec((tm, D), lambda i, k: (i, 0))
    if mode == "mod":
        in_specs += [_mod_spec(layer + 1, 0, tpb, ctx, 2), _mod_spec(layer + 1, 1, tpb, ctx, 2)]
        args += [mods5, mods5]
        out_shape = (jax.ShapeDtypeStruct((t, D), F32), jax.ShapeDtypeStruct((t, D), BF16))
        out_specs = (row_spec, row_spec)
    else:
        out_shape = jax.ShapeDtypeStruct((t, D), F32)
        out_specs = row_spec
    return pl.pallas_call(
        functools.partial(_ffn_down_kernel, nk=nk, mode=mode),
        out_shape=out_shape,
        grid=(t // tm, nk),
        in_specs=in_specs,
        out_specs=out_specs,
        scratch_shapes=[pltpu.VMEM((tm, D), F32)],
        compiler_params=_cp(("arbitrary", "arbitrary"), 52),
        name="ffn_down",
    )(*args)


def kernel(x, c, ctx, c_ctx, w_ada, b_ada, g_mix, w_in, g_q, g_k, w_gmlp, b_gmlp, g_gmlp_v,
           w_sconv, w_pa, w_pb, w_pc, w_o, g_ffn, w_up, w_ffn_conv, w_down, g_final):
    pad = D_FF_PAD - D_FF
    cvec = jnp.concatenate([c, c_ctx[None, :], jnp.zeros((3, D), F32)], axis=0)
    mods5 = _mods(cvec, w_ada, b_ada).reshape(DEPTH, 8, 6, 1, D)
    rope_tabs = _rope_tables()
    q_scale = HD ** -0.5 * math.log2(math.e)

    xs = x.reshape(BATCH * SEQ, D)
    cs = ctx.reshape(BATCH * CTX, D)
    hx = _normmod(xs, g_mix[0], mods5, 0, False)
    hc = _normmod(cs, g_mix[0], mods5, 0, True)

    out = None
    for l in range(DEPTH):
        last = l == DEPTH - 1
        w_in_l = w_in[l].astype(BF16)
        ws = w_gmlp[l].astype(BF16)
        bs = jnp.broadcast_to(b_gmlp[l][:, :, None], (GG, CHUNK, CHUNK))
        wpa, wpb, wpc, wo = (w_pa[l].astype(BF16), w_pb[l].astype(BF16),
                             w_pc[l].astype(BF16), w_o[l].astype(BF16))
        wua = jnp.pad(w_up[l][:, :D_FF], ((0, 0), (0, pad))).astype(BF16)
        wub = jnp.pad(w_up[l][:, D_FF:], ((0, 0), (0, pad))).astype(BF16)
        cva = jnp.pad(w_ffn_conv[l][:, :D_FF], ((0, 0), (0, pad)))
        cvb = jnp.pad(w_ffn_conv[l][:, D_FF:], ((0, 0), (0, pad)))
        wdn = jnp.pad(w_down[l], ((0, pad), (0, 0))).astype(BF16)

        px = _inproj(hx, w_in_l, 0, IN_DIM)
        if not last:
            pc = _inproj(hc, w_in_l, 0, IN_DIM)
            kc = _headnorm(pc, OFF_K // KV_DIM, N_KV, g_k[l], 1.0, None)
            vc = pc[:, OFF_V:OFF_V + KV_DIM]
        else:
            pkv = _inproj(hc, w_in_l, OFF_K // TN_IN, 2 * KV_DIM)
            kc = _headnorm(pkv, 0, N_KV, g_k[l], 1.0, None)
            vc = pkv[:, KV_DIM:]
        qx = _headnorm(px, 0, N_HEADS, g_q[l], q_scale, rope_tabs)
        kx = _headnorm(px, OFF_K // KV_DIM, N_KV, g_k[l], 1.0, rope_tabs)
        vx = px[:, OFF_V:OFF_V + KV_DIM]
        kc3 = kc.reshape(BATCH, CTX, KV_DIM)
        vc3 = vc.reshape(BATCH, CTX, KV_DIM)
        k_all = jnp.concatenate([kx.reshape(BATCH, SEQ, KV_DIM), kc3], axis=1)
        v_all = jnp.concatenate([vx.reshape(BATCH, SEQ, KV_DIM), vc3], axis=1)
        attn_x = _attention(qx, k_all, v_all, SEQ)
        ya = _gmlp(px, g_gmlp_v[l], ws, bs)
        yb = _sconv(px, w_sconv[l], SEQ, SEQ)
        m = _merge(ya, yb, attn_x, px, wpa, wpb, wpc)
        x1, fx = _outproj(m, wo, xs, g_ffn[l], mods5, l, False)

        gx = _ffn_up(fx, wua, wub, cva, cvb, SEQ, SEQ)
        if last:
            out = _ffn_down(gx, wdn, x1, g_final, mods5, l, False, "final")
        else:
            xs, hx = _ffn_down(gx, wdn, x1, g_mix[l + 1], mods5, l, False, "mod")

            qc = _headnorm(pc, 0, N_HEADS, g_q[l], q_scale, None)
            attn_c = _attention(qc, kc3, vc3, CTX)
            yac = _gmlp(pc, g_gmlp_v[l], ws, bs)
            ybc = _sconv(pc, w_sconv[l], CTX, BATCH * CTX)
            mc = _merge(yac, ybc, attn_c, pc, wpa, wpb, wpc)
            c1, fc = _outproj(mc, wo, cs, g_ffn[l], mods5, l, True)
            gc = _ffn_up(fc, wua, wub, cva, cvb, CTX, BATCH * CTX)
            cs, hc = _ffn_down(gc, wdn, c1, g_mix[l + 1], mods5, l, True, "mod")

    return out.reshape(BATCH, SEQ, D)
```

```python
import functools
import math

import jax
import jax.numpy as jnp
from jax.experimental import pallas as pl
from jax.experimental.pallas import tpu as pltpu

F32 = jnp.float32
BF16 = jnp.bfloat16

D = 2048
BATCH = 4
SEQ = 2048
CTX = 256
DEPTH = 2
GRID_W = 64
N_HEADS = 16
N_KV = 4
HD = 128
GROUP = N_HEADS // N_KV
CHUNK = 128
GW = D // 2
GG = 8
CW = D // 2
D_FF = 5504
EPS = 1e-6
ROPE_THETA = 10000.0
KV_DIM = N_KV * HD
IN_DIM = D + 2 * KV_DIM + 2 * GW + 3 * CW + 3 * D
OFF_K = D
OFF_U = D + 2 * KV_DIM
OFF_B = OFF_U + 2 * GW
OFF_G = OFF_B + 3 * CW
R_K = D
R_V = D + KV_DIM
R_B = D + 2 * KV_DIM
R_C = R_B + CW
R_H = R_C + CW
LANES = 128

VMEM_PHYS_V7X = 64 * 1024 * 1024
VMEM_CAP = VMEM_PHYS_V7X - 8 * 1024 * 1024


def _cp(sem, vmem_mb):
    return pltpu.CompilerParams(
        dimension_semantics=sem,
        vmem_limit_bytes=min(int(vmem_mb * 1024 * 1024), VMEM_CAP))


def _rms(xf, g):
    ms = jnp.mean(xf * xf, axis=-1, keepdims=True)
    return xf * jax.lax.rsqrt(ms + EPS) * g


def _sigmoid(x):
    return 0.5 * jnp.tanh(0.5 * x) + 0.5


def _gelu_tanh(x):
    c = math.sqrt(2.0 / math.pi)
    return 0.5 * x * (1.0 + jnp.tanh(c * (x + 0.044715 * (x * x * x))))


def _mods_kernel(c_ref, w_ref, b_ref, o_ref):
    c = c_ref[...]
    s = (c * _sigmoid(c)).astype(BF16)
    w = w_ref[...].astype(BF16)
    o_ref[...] = jnp.dot(s, w, preferred_element_type=F32) + b_ref[...]


def _mods(cvec, w_ada, b_ada):
    tn = 1024
    nj = 6 * D // tn
    return pl.pallas_call(
        _mods_kernel,
        out_shape=jax.ShapeDtypeStruct((DEPTH, 8, 6 * D), F32),
        grid=(DEPTH, nj),
        in_specs=[
            pl.BlockSpec((8, D), lambda l, j: (0, 0)),
            pl.BlockSpec((None, D, tn), lambda l, j: (l, 0, j)),
            pl.BlockSpec((None, 1, tn), lambda l, j: (l, 0, j)),
        ],
        out_specs=pl.BlockSpec((None, 8, tn), lambda l, j: (l, 0, j)),
        compiler_params=_cp(("arbitrary", "arbitrary"), 40),
        name="adaln_mods",
    )(cvec, w_ada, b_ada.reshape(DEPTH, 1, 6 * D))


def _mod_spec(layer, k, row_of, width=D, col_of=None):
    if col_of is None:
        return pl.BlockSpec((None, None, None, 1, width),
                            lambda *g: (layer, row_of(*g), k, 0, 0))
    return pl.BlockSpec((None, None, None, 1, width),
                        lambda *g: (layer, row_of(*g), k, 0, col_of(*g)))


def _row_of(ctx, tiles_per_batch, axis):
    if ctx:
        return lambda *g: BATCH
    return lambda *g: g[axis] // tiles_per_batch


def _normmod_kernel(x_ref, g_ref, sh_ref, sc_ref, o_ref):
    y = _rms(x_ref[...], g_ref[...])
    o_ref[...] = (y * (1.0 + sc_ref[...]) + sh_ref[...]).astype(o_ref.dtype)


def _norm_kernel(x_ref, g_ref, o_ref):
    o_ref[...] = _rms(x_ref[...], g_ref[...]).astype(o_ref.dtype)


def _normmod(x, g, mods5, layer, ctx):
    t = x.shape[0]
    tm = 512
    row_of = _row_of(ctx, SEQ // tm, 0)
    return pl.pallas_call(
        _normmod_kernel,
        out_shape=jax.ShapeDtypeStruct((t, D), BF16),
        grid=(t // tm,),
        in_specs=[
            pl.BlockSpec((tm, D), lambda i: (i, 0)),
            pl.BlockSpec((1, D), lambda i: (0, 0)),
            _mod_spec(layer, 0, row_of),
            _mod_spec(layer, 1, row_of),
        ],
        out_specs=pl.BlockSpec((tm, D), lambda i: (i, 0)),
        compiler_params=_cp(("arbitrary",), 32),
        name="norm_mod",
    )(x, g.reshape(1, D), mods5, mods5)


def _final_norm(x, g):
    t = x.shape[0]
    tm = 512
    return pl.pallas_call(
        _norm_kernel,
        out_shape=jax.ShapeDtypeStruct((t, D), F32),
        grid=(t // tm,),
        in_specs=[
            pl.BlockSpec((tm, D), lambda i: (i, 0)),
            pl.BlockSpec((1, D), lambda i: (0, 0)),
        ],
        out_specs=pl.BlockSpec((tm, D), lambda i: (i, 0)),
        compiler_params=_cp(("arbitrary",), 32),
        name="final_norm",
    )(x, g.reshape(1, D))


TN_IN = 1024
_ACTS = {"raw": lambda a: a, "gelu": _gelu_tanh, "sigmoid": _sigmoid}


def _inproj_kernel(h_ref, w_ref, o_ref, wb_ref, *, act):
    @pl.when(pl.program_id(1) == 0)
    def _():
        wb_ref[...] = w_ref[...].astype(BF16)

    acc = jnp.dot(h_ref[...], wb_ref[...], preferred_element_type=F32)
    o_ref[...] = _ACTS[act](acc).astype(o_ref.dtype)


def _inproj(h, w, layer, col_tiles, act):
    t = h.shape[0]
    tm = 1024
    first, nj, skip_at, skip = col_tiles

    def wmap(j, i):
        return (layer, 0, first + j + skip * (j // skip_at))

    return pl.pallas_call(
        functools.partial(_inproj_kernel, act=act),
        out_shape=jax.ShapeDtypeStruct((t, nj * TN_IN), BF16),
        grid=(nj, t // tm),
        in_specs=[
            pl.BlockSpec((tm, D), lambda j, i: (i, 0)),
            pl.BlockSpec((None, D, TN_IN), wmap),
        ],
        out_specs=pl.BlockSpec((tm, TN_IN), lambda j, i: (i, j)),
        scratch_shapes=[pltpu.VMEM((D, TN_IN), BF16)],
        compiler_params=_cp(("arbitrary", "arbitrary"), 48),
        name="in_proj_" + act,
    )(h, w)


COLS_RAW = (0, 6, 3, (OFF_B - OFF_U) // TN_IN)
COLS_GELU = (OFF_U // TN_IN, (OFF_B - OFF_U) // TN_IN, 1 << 20, 0)
COLS_GATE = (OFF_G // TN_IN, 3 * D // TN_IN, 1 << 20, 0)
COLS_KV = (OFF_K // TN_IN, 1, 1 << 20, 0)


def _rope_tables():
    rows = SEQ // GRID_W
    row = jnp.repeat(jnp.arange(rows, dtype=F32), GRID_W)
    col = jnp.tile(jnp.arange(GRID_W, dtype=F32), rows)
    n_freq = HD // 4
    inv_freq = ROPE_THETA ** (-jnp.arange(n_freq, dtype=F32) / n_freq)
    ar = row[:, None] * inv_freq
    ac = col[:, None] * inv_freq
    cos = jnp.concatenate([jnp.cos(ar), jnp.cos(ar), jnp.cos(ac), jnp.cos(ac)], axis=-1)
    sin = jnp.concatenate([-jnp.sin(ar), jnp.sin(ar), -jnp.sin(ac), jnp.sin(ac)], axis=-1)
    return cos, sin


def _headnorm_kernel(x_ref, g_ref, *rest, n_heads, scale, rope):
    if rope:
        cos_ref, sin_ref, o_ref = rest
        cos = cos_ref[...]
        sin = sin_ref[...]
        lane = jax.lax.broadcasted_iota(jnp.int32, cos.shape, 1)
        first_half = (lane % (HD // 2)) < (HD // 4)
    else:
        (o_ref,) = rest
    g = g_ref[...] * scale
    for h in range(n_heads):
        x = x_ref[:, h * HD:(h + 1) * HD].astype(F32)
        y = _rms(x, g)
        if rope:
            partner = jnp.where(first_half,
                                pltpu.roll(y, HD - HD // 4, axis=1),
                                pltpu.roll(y, HD // 4, axis=1))
            y = y * cos + partner * sin
        o_ref[:, h * HD:(h + 1) * HD] = y.astype(o_ref.dtype)


def _headnorm(src, col_blk, n_heads, g, scale, rope_tabs):
    t = src.shape[0]
    tm = 512
    w = n_heads * HD
    rope = rope_tabs is not None
    in_specs = [
        pl.BlockSpec((tm, w), lambda i: (i, col_blk)),
        pl.BlockSpec((1, HD), lambda i: (0, 0)),
    ]
    args = [src, g.reshape(1, HD)]
    if rope:
        nt = SEQ // tm
        in_specs += [pl.BlockSpec((tm, HD), lambda i: (i % nt, 0))] * 2
        args += list(rope_tabs)
    return pl.pallas_call(
        functools.partial(_headnorm_kernel, n_heads=n_heads, scale=scale, rope=rope),
        out_shape=jax.ShapeDtypeStruct((t, w), BF16),
        grid=(t // tm,),
        in_specs=in_specs,
        out_specs=pl.BlockSpec((tm, w), lambda i: (i, 0)),
        compiler_params=_cp(("arbitrary",), 32),
        name="head_norm",
    )(*args)


def _attn_kernel(q_ref, k_ref, v_ref, o_ref):
    k = k_ref[...]
    v = v_ref[...]
    for g in range(GROUP):
        q = q_ref[:, g * HD:(g + 1) * HD]
        s = jax.lax.dot_general(q, k, (((1,), (1,)), ((), ())),
                                preferred_element_type=F32)
        m = jnp.max(s, axis=-1, keepdims=True)
        p = jnp.exp2(s - m)
        l = jnp.sum(p, axis=-1, keepdims=True)
        o = jnp.dot(p.astype(BF16), v, preferred_element_type=F32)
        o_ref[:, g * HD:(g + 1) * HD] = (o / l).astype(o_ref.dtype)


def _attention(q, k, v, lq):
    s_len = k.shape[1]
    tq = 256
    nq = lq // tq
    return pl.pallas_call(
        _attn_kernel,
        out_shape=jax.ShapeDtypeStruct((BATCH * lq, D), BF16),
        grid=(BATCH, N_KV, nq),
        in_specs=[
            pl.BlockSpec((tq, GROUP * HD), lambda b, h, i: (b * nq + i, h)),
            pl.BlockSpec((None, s_len, HD), lambda b, h, i: (b, 0, h)),
            pl.BlockSpec((None, s_len, HD), lambda b, h, i: (b, 0, h)),
        ],
        out_specs=pl.BlockSpec((tq, GROUP * HD), lambda b, h, i: (b * nq + i, h)),
        compiler_params=_cp(("arbitrary", "arbitrary", "arbitrary"), 40),
        name="attention",
    )(q, k, v)


def _gmlp_kernel(u_ref, v_ref, gv_ref, ws_ref, bs_ref, o_ref, *, tm):
    v = _rms(v_ref[...].astype(F32), gv_ref[...]).astype(BF16)
    for g in range(GG):
        c0 = g * CHUNK
        w = ws_ref[g].astype(BF16)
        for n in range(tm // CHUNK):
            r0 = n * CHUNK
            mixed = jnp.dot(w, v[r0:r0 + CHUNK, c0:c0 + CHUNK],
                            preferred_element_type=F32) + bs_ref[g]
            u = u_ref[r0:r0 + CHUNK, c0:c0 + CHUNK].astype(F32)
            o_ref[r0:r0 + CHUNK, c0:c0 + CHUNK] = (u * mixed).astype(o_ref.dtype)


def _gmlp(zg, gv, ws, bs):
    t = zg.shape[0]
    tm = 512
    return pl.pallas_call(
        functools.partial(_gmlp_kernel, tm=tm),
        out_shape=jax.ShapeDtypeStruct((t, GW), BF16),
        grid=(t // tm,),
        in_specs=[
            pl.BlockSpec((tm, GW), lambda i: (i, 0)),
            pl.BlockSpec((tm, GW), lambda i: (i, 1)),
            pl.BlockSpec((1, GW), lambda i: (0, 0)),
            pl.BlockSpec((GG, CHUNK, CHUNK), lambda i: (0, 0, 0)),
            pl.BlockSpec((GG, CHUNK, CHUNK), lambda i: (0, 0, 0)),
        ],
        out_specs=pl.BlockSpec((tm, GW), lambda i: (i, 0)),
        compiler_params=_cp(("arbitrary",), 32),
        name="gmlp",
    )(zg, zg, gv.reshape(1, GW), ws, bs)


def _dwconv3(x, w, seq_len):
    rows = x.shape[0]
    pos = jax.lax.broadcasted_iota(jnp.int32, x.shape, 0) % seq_len
    prev = jnp.where(pos == 0, 0.0, pltpu.roll(x, 1, axis=0))
    nxt = jnp.where(pos == seq_len - 1, 0.0, pltpu.roll(x, rows - 1, axis=0))
    return prev * w[0:1, :] + x * w[1:2, :] + nxt * w[2:3, :]


def _sconv_kernel(b_ref, c_ref, h_ref, w_ref, o_ref, *, seq_len):
    t = c_ref[...].astype(F32) * h_ref[...].astype(F32)
    y = b_ref[...].astype(F32) * _dwconv3(t, w_ref[...], seq_len)
    o_ref[...] = y.astype(o_ref.dtype)


def _sconv(raw, w, seq_len, rows):
    t = raw.shape[0]
    tc = 256
    return pl.pallas_call(
        functools.partial(_sconv_kernel, seq_len=seq_len),
        out_shape=jax.ShapeDtypeStruct((t, CW), BF16),
        grid=(t // rows, CW // tc),
        in_specs=[
            pl.BlockSpec((rows, tc), lambda i, j: (i, R_B // tc + j)),
            pl.BlockSpec((rows, tc), lambda i, j: (i, R_C // tc + j)),
            pl.BlockSpec((rows, tc), lambda i, j: (i, R_H // tc + j)),
            pl.BlockSpec((3, tc), lambda i, j: (0, j)),
        ],
        out_specs=pl.BlockSpec((rows, tc), lambda i, j: (i, j)),
        compiler_params=_cp(("arbitrary", "arbitrary"), 40),
        name="short_conv",
    )(raw, raw, raw, w)


def _merge_kernel(ya_ref, yb_ref, at_ref, wa_ref, wb_ref, wc_ref,
                  ga_ref, gb_ref, gc_ref, o_ref, wa_s, wb_s, wc_s):
    @pl.when(pl.program_id(1) == 0)
    def _():
        wa_s[...] = wa_ref[...].astype(BF16)
        wb_s[...] = wb_ref[...].astype(BF16)
        wc_s[...] = wc_ref[...].astype(BF16)

    a = jnp.dot(ya_ref[...], wa_s[...], preferred_element_type=F32)
    m = ga_ref[...].astype(F32) * a
    b = jnp.dot(yb_ref[...], wb_s[...], preferred_element_type=F32)
    m = m + gb_ref[...].astype(F32) * b
    c = jnp.dot(at_ref[...], wc_s[...], preferred_element_type=F32)
    m = m + gc_ref[...].astype(F32) * c
    o_ref[...] = m.astype(o_ref.dtype)


def _merge(ya, yb, attn, gates, w_pa, w_pb, w_pc, layer):
    t = ya.shape[0]
    tm = 512
    tn = 512
    nb = D // tn
    return pl.pallas_call(
        _merge_kernel,
        out_shape=jax.ShapeDtypeStruct((t, D), BF16),
        grid=(nb, t // tm),
        in_specs=[
            pl.BlockSpec((tm, GW), lambda j, i: (i, 0)),
            pl.BlockSpec((tm, CW), lambda j, i: (i, 0)),
            pl.BlockSpec((tm, D), lambda j, i: (i, 0)),
            pl.BlockSpec((None, GW, tn), lambda j, i: (layer, 0, j)),
            pl.BlockSpec((None, CW, tn), lambda j, i: (layer, 0, j)),
            pl.BlockSpec((None, D, tn), lambda j, i: (layer, 0, j)),
            pl.BlockSpec((tm, tn), lambda j, i: (i, j)),
            pl.BlockSpec((tm, tn), lambda j, i: (i, nb + j)),
            pl.BlockSpec((tm, tn), lambda j, i: (i, 2 * nb + j)),
        ],
        out_specs=pl.BlockSpec((tm, tn), lambda j, i: (i, j)),
        scratch_shapes=[pltpu.VMEM((GW, tn), BF16), pltpu.VMEM((CW, tn), BF16),
                        pltpu.VMEM((D, tn), BF16)],
        compiler_params=_cp(("arbitrary", "arbitrary"), 48),
        name="merge",
    )(ya, yb, attn, w_pa, w_pb, w_pc, gates, gates, gates)


def _outproj_kernel(m_ref, w_ref, x_ref, gt_ref, g_ref, sh_ref, sc_ref, x1_ref, h_ref, w_s):
    @pl.when(pl.program_id(0) == 0)
    def _():
        w_s[...] = w_ref[...].astype(BF16)

    y = jnp.dot(m_ref[...], w_s[...], preferred_element_type=F32)
    x1 = x_ref[...] + gt_ref[...] * y
    x1_ref[...] = x1
    h_ref[...] = (_rms(x1, g_ref[...]) * (1.0 + sc_ref[...]) + sh_ref[...]).astype(h_ref.dtype)


def _outproj(m, w_o, x, g_ffn, mods5, layer, ctx):
    t = m.shape[0]
    tm = 256
    row_of = _row_of(ctx, SEQ // tm, 0)
    return pl.pallas_call(
        _outproj_kernel,
        out_shape=(jax.ShapeDtypeStruct((t, D), F32), jax.ShapeDtypeStruct((t, D), BF16)),
        grid=(t // tm,),
        in_specs=[
            pl.BlockSpec((tm, D), lambda i: (i, 0)),
            pl.BlockSpec((None, D, D), lambda i: (layer, 0, 0), pipeline_mode=pl.Buffered(1)),
            pl.BlockSpec((tm, D), lambda i: (i, 0)),
            _mod_spec(layer, 2, row_of),
            pl.BlockSpec((1, D), lambda i: (0, 0)),
            _mod_spec(layer, 3, row_of),
            _mod_spec(layer, 4, row_of),
        ],
        out_specs=(pl.BlockSpec((tm, D), lambda i: (i, 0)),
                   pl.BlockSpec((tm, D), lambda i: (i, 0))),
        scratch_shapes=[pltpu.VMEM((D, D), BF16)],
        compiler_params=_cp(("arbitrary",), 52),
        name="out_proj",
    )(m, w_o, x, mods5, g_ffn.reshape(1, D), mods5, mods5)


FF_BLOCKS = D_FF // LANES
TN_FF = 2 * LANES


def _ffn_up_kernel(h_ref, a0_ref, a1_ref, b0_ref, b1_ref, ca0_ref, ca1_ref, cb0_ref, cb1_ref,
                   o_ref, *, seq_len):
    h = h_ref[...]
    wa = jnp.concatenate([a0_ref[...], a1_ref[...]], axis=1).astype(BF16)
    wb = jnp.concatenate([b0_ref[...], b1_ref[...]], axis=1).astype(BF16)
    ca = jnp.concatenate([ca0_ref[...], ca1_ref[...]], axis=1)
    cb = jnp.concatenate([cb0_ref[...], cb1_ref[...]], axis=1)
    a = _dwconv3(jnp.dot(h, wa, preferred_element_type=F32), ca, seq_len)
    b = _dwconv3(jnp.dot(h, wb, preferred_element_type=F32), cb, seq_len)
    o_ref[...] = (a * _sigmoid(a) * b).astype(o_ref.dtype)


def _ffn_up(h, w_up, w_conv, layer, seq_len, rows):
    t = h.shape[0]
    nj = pl.cdiv(D_FF, TN_FF)
    last = 2 * FF_BLOCKS - 1

    def wspec(rows_, off, k):
        return pl.BlockSpec((None, rows_, LANES),
                            lambda i, j: (layer, 0, jnp.minimum(off + 2 * j + k, last)))

    return pl.pallas_call(
        functools.partial(_ffn_up_kernel, seq_len=seq_len),
        out_shape=jax.ShapeDtypeStruct((t, D_FF), BF16),
        grid=(t // rows, nj),
        in_specs=[
            pl.BlockSpec((rows, D), lambda i, j: (i, 0)),
            wspec(D, 0, 0), wspec(D, 0, 1), wspec(D, FF_BLOCKS, 0), wspec(D, FF_BLOCKS, 1),
            wspec(3, 0, 0), wspec(3, 0, 1), wspec(3, FF_BLOCKS, 0), wspec(3, FF_BLOCKS, 1),
        ],
        out_specs=pl.BlockSpec((rows, TN_FF), lambda i, j: (i, j)),
        compiler_params=_cp(("arbitrary", "arbitrary"), 52),
        name="ffn_up",
    )(h, w_up, w_up, w_up, w_up, w_conv, w_conv, w_conv, w_conv)


def _ffn_down_kernel(g_ref, w_ref, x_ref, gt_ref, o_ref, w_s):
    @pl.when(pl.program_id(1) == 0)
    def _():
        w_s[...] = w_ref[...].astype(BF16)

    y = jnp.dot(g_ref[...], w_s[...], preferred_element_type=F32)
    o_ref[...] = x_ref[...] + gt_ref[...] * y


def _ffn_down(g, w_down, x, mods5, layer, ctx):
    t = g.shape[0]
    tm = 512
    tn = 512
    row_of = _row_of(ctx, SEQ // tm, 1)
    return pl.pallas_call(
        _ffn_down_kernel,
        out_shape=jax.ShapeDtypeStruct((t, D), F32),
        grid=(D // tn, t // tm),
        in_specs=[
            pl.BlockSpec((tm, D_FF), lambda j, i: (i, 0)),
            pl.BlockSpec((None, D_FF, tn), lambda j, i: (layer, 0, j)),
            pl.BlockSpec((tm, tn), lambda j, i: (i, j)),
            _mod_spec(layer, 5, row_of, tn, lambda j, i: j),
        ],
        out_specs=pl.BlockSpec((tm, tn), lambda j, i: (i, j)),
        scratch_shapes=[pltpu.VMEM((D_FF, tn), BF16)],
        compiler_params=_cp(("arbitrary", "arbitrary"), 54),
        name="ffn_down",
    )(g, w_down, x, mods5)


def kernel(x, c, ctx, c_ctx, w_ada, b_ada, g_mix, w_in, g_q, g_k, w_gmlp, b_gmlp, g_gmlp_v,
           w_sconv, w_pa, w_pb, w_pc, w_o, g_ffn, w_up, w_ffn_conv, w_down, g_final):
    cvec = jnp.concatenate([c, c_ctx[None, :], jnp.zeros((3, D), F32)], axis=0)
    mods5 = _mods(cvec, w_ada, b_ada).reshape(DEPTH, 8, 6, 1, D)
    rope_tabs = _rope_tables()
    q_scale = HD ** -0.5 * math.log2(math.e)

    xs = x.reshape(BATCH * SEQ, D)
    cs = ctx.reshape(BATCH * CTX, D)

    for l in range(DEPTH):
        last = l == DEPTH - 1
        bs = jnp.broadcast_to(b_gmlp[l][:, :, None], (GG, CHUNK, CHUNK))
        hx = _normmod(xs, g_mix[l], mods5, l, False)
        hc = _normmod(cs, g_mix[l], mods5, l, True)

        raw_x = _inproj(hx, w_in, l, COLS_RAW, "raw")
        zg_x = _inproj(hx, w_in, l, COLS_GELU, "gelu")
        gates_x = _inproj(hx, w_in, l, COLS_GATE, "sigmoid")
        if not last:
            raw_c = _inproj(hc, w_in, l, COLS_RAW, "raw")
            kc = _headnorm(raw_c, R_K // KV_DIM, N_KV, g_k[l], 1.0, None)
            vc = raw_c[:, R_V:R_V + KV_DIM]
        else:
            kv_c = _inproj(hc, w_in, l, COLS_KV, "raw")
            kc = _headnorm(kv_c, 0, N_KV, g_k[l], 1.0, None)
            vc = kv_c[:, KV_DIM:]
        qx = _headnorm(raw_x, 0, N_HEADS, g_q[l], q_scale, rope_tabs)
        kx = _headnorm(raw_x, R_K // KV_DIM, N_KV, g_k[l], 1.0, rope_tabs)
        vx = raw_x[:, R_V:R_V + KV_DIM]
        kc3 = kc.reshape(BATCH, CTX, KV_DIM)
        vc3 = vc.reshape(BATCH, CTX, KV_DIM)
        k_all = jnp.concatenate([kx.reshape(BATCH, SEQ, KV_DIM), kc3], axis=1)
        v_all = jnp.concatenate([vx.reshape(BATCH, SEQ, KV_DIM), vc3], axis=1)
        attn_x = _attention(qx, k_all, v_all, SEQ)
        ya = _gmlp(zg_x, g_gmlp_v[l], w_gmlp[l], bs)
        yb = _sconv(raw_x, w_sconv[l], SEQ, SEQ)
        m = _merge(ya, yb, attn_x, gates_x, w_pa, w_pb, w_pc, l)
        x1, fx = _outproj(m, w_o, xs, g_ffn[l], mods5, l, False)

        gx = _ffn_up(fx, w_up, w_ffn_conv, l, SEQ, SEQ)
        xs = _ffn_down(gx, w_down, x1, mods5, l, False)

        if not last:
            zg_c = _inproj(hc, w_in, l, COLS_GELU, "gelu")
            gates_c = _inproj(hc, w_in, l, COLS_GATE, "sigmoid")
            qc = _headnorm(raw_c, 0, N_HEADS, g_q[l], q_scale, None)
            attn_c = _attention(qc, kc3, vc3, CTX)
            yac = _gmlp(zg_c, g_gmlp_v[l], w_gmlp[l], bs)
            ybc = _sconv(raw_c, w_sconv[l], CTX, BATCH * CTX)
            mc = _merge(yac, ybc, attn_c, gates_c, w_pa, w_pb, w_pc, l)
            c1, fc = _outproj(mc, w_o, cs, g_ffn[l], mods5, l, True)
            gc = _ffn_up(fc, w_up, w_ffn_conv, l, CTX, BATCH * CTX)
            cs = _ffn_down(gc, w_down, c1, mods5, l, True)

    return _final_norm(xs, g_final).reshape(BATCH, SEQ, D)
```

```python
import functools
import math

import jax
import jax.numpy as jnp
from jax.experimental import pallas as pl
from jax.experimental.pallas import tpu as pltpu

F32 = jnp.float32
BF16 = jnp.bfloat16

D = 2048
BATCH = 4
SEQ = 2048
CTX = 256
DEPTH = 2
GRID_W = 64
N_HEADS = 16
N_KV = 4
HD = 128
GROUP = N_HEADS // N_KV
CHUNK = 128
GW = D // 2
GG = 8
CW = D // 2
D_FF = 5504
EPS = 1e-6
ROPE_THETA = 10000.0
KV_DIM = N_KV * HD
IN_DIM = D + 2 * KV_DIM + 2 * GW + 3 * CW + 3 * D
OFF_K = D
OFF_U = D + 2 * KV_DIM
OFF_B = OFF_U + 2 * GW
OFF_G = OFF_B + 3 * CW
R_K = D
R_V = D + KV_DIM
R_B = D + 2 * KV_DIM
R_C = R_B + CW
R_H = R_C + CW
LANES = 128

VMEM_PHYS_V7X = 64 * 1024 * 1024
VMEM_CAP = VMEM_PHYS_V7X - 8 * 1024 * 1024


def _cp(sem, vmem_mb):
    return pltpu.CompilerParams(
        dimension_semantics=sem,
        vmem_limit_bytes=min(int(vmem_mb * 1024 * 1024), VMEM_CAP))


def _rms(xf, g):
    ms = jnp.mean(xf * xf, axis=-1, keepdims=True)
    return xf * jax.lax.rsqrt(ms + EPS) * g


def _sigmoid(x):
    return 0.5 * jnp.tanh(0.5 * x) + 0.5


def _gelu_tanh(x):
    c = math.sqrt(2.0 / math.pi)
    return 0.5 * x * (1.0 + jnp.tanh(c * (x + 0.044715 * (x * x * x))))


def _mods_kernel(c_ref, w_ref, b_ref, o_ref):
    c = c_ref[...]
    s = (c * _sigmoid(c)).astype(BF16)
    w = w_ref[...].astype(BF16)
    o_ref[...] = jnp.dot(s, w, preferred_element_type=F32) + b_ref[...]


def _mods(cvec, w_ada, b_ada):
    tn = 1024
    nj = 6 * D // tn
    return pl.pallas_call(
        _mods_kernel,
        out_shape=jax.ShapeDtypeStruct((DEPTH, 8, 6 * D), F32),
        grid=(DEPTH, nj),
        in_specs=[
            pl.BlockSpec((8, D), lambda l, j: (0, 0)),
            pl.BlockSpec((None, D, tn), lambda l, j: (l, 0, j)),
            pl.BlockSpec((None, 1, tn), lambda l, j: (l, 0, j)),
        ],
        out_specs=pl.BlockSpec((None, 8, tn), lambda l, j: (l, 0, j)),
        compiler_params=_cp(("arbitrary", "arbitrary"), 40),
        name="adaln_mods",
    )(cvec, w_ada, b_ada.reshape(DEPTH, 1, 6 * D))


def _mod_spec(layer, k, row_of, width=D, col_of=None):
    if col_of is None:
        return pl.BlockSpec((None, None, None, 1, width),
                            lambda *g: (layer, row_of(*g), k, 0, 0))
    return pl.BlockSpec((None, None, None, 1, width),
                        lambda *g: (layer, row_of(*g), k, 0, col_of(*g)))


def _row_of(ctx, tiles_per_batch, axis):
    if ctx:
        return lambda *g: BATCH
    return lambda *g: g[axis] // tiles_per_batch


def _normmod_kernel(x_ref, g_ref, sh_ref, sc_ref, o_ref):
    y = _rms(x_ref[...], g_ref[...])
    o_ref[...] = (y * (1.0 + sc_ref[...]) + sh_ref[...]).astype(o_ref.dtype)


def _norm_kernel(x_ref, g_ref, o_ref):
    o_ref[...] = _rms(x_ref[...], g_ref[...]).astype(o_ref.dtype)


def _normmod(x, g, mods5, layer, ctx):
    t = x.shape[0]
    tm = 1024
    row_of = _row_of(ctx, SEQ // tm, 0)
    return pl.pallas_call(
        _normmod_kernel,
        out_shape=jax.ShapeDtypeStruct((t, D), BF16),
        grid=(t // tm,),
        in_specs=[
            pl.BlockSpec((tm, D), lambda i: (i, 0)),
            pl.BlockSpec((1, D), lambda i: (0, 0)),
            _mod_spec(layer, 0, row_of),
            _mod_spec(layer, 1, row_of),
        ],
        out_specs=pl.BlockSpec((tm, D), lambda i: (i, 0)),
        compiler_params=_cp(("arbitrary",), 44),
        name="norm_mod",
    )(x, g.reshape(1, D), mods5, mods5)


def _final_norm(x, g):
    t = x.shape[0]
    tm = 1024
    return pl.pallas_call(
        _norm_kernel,
        out_shape=jax.ShapeDtypeStruct((t, D), F32),
        grid=(t // tm,),
        in_specs=[
            pl.BlockSpec((tm, D), lambda i: (i, 0)),
            pl.BlockSpec((1, D), lambda i: (0, 0)),
        ],
        out_specs=pl.BlockSpec((tm, D), lambda i: (i, 0)),
        compiler_params=_cp(("arbitrary",), 48),
        name="final_norm",
    )(x, g.reshape(1, D))


TN_IN = 1024
_ACTS = {"raw": lambda a: a, "gelu": _gelu_tanh, "sigmoid": _sigmoid}


def _inproj_kernel(h_ref, w_ref, o_ref, wb_ref, *, act):
    @pl.when(pl.program_id(1) == 0)
    def _():
        wb_ref[...] = w_ref[...].astype(BF16)

    acc = jnp.dot(h_ref[...], wb_ref[...], preferred_element_type=F32)
    o_ref[...] = _ACTS[act](acc).astype(o_ref.dtype)


def _inproj(h, w, layer, col_tiles, act):
    t = h.shape[0]
    tm = 1024
    first, nj, skip_at, skip = col_tiles

    def wmap(j, i):
        return (layer, 0, first + j + skip * (j // skip_at))

    return pl.pallas_call(
        functools.partial(_inproj_kernel, act=act),
        out_shape=jax.ShapeDtypeStruct((t, nj * TN_IN), BF16),
        grid=(nj, t // tm),
        in_specs=[
            pl.BlockSpec((tm, D), lambda j, i: (i, 0)),
            pl.BlockSpec((None, D, TN_IN), wmap),
        ],
        out_specs=pl.BlockSpec((tm, TN_IN), lambda j, i: (i, j)),
        scratch_shapes=[pltpu.VMEM((D, TN_IN), BF16)],
        compiler_params=_cp(("arbitrary", "arbitrary"), 48),
        name="in_proj_" + act,
    )(h, w)


COLS_RAW = (0, 6, 3, (OFF_B - OFF_U) // TN_IN)
COLS_GELU = (OFF_U // TN_IN, (OFF_B - OFF_U) // TN_IN, 1 << 20, 0)
COLS_GATE = (OFF_G // TN_IN, 3 * D // TN_IN, 1 << 20, 0)
COLS_KV = (OFF_K // TN_IN, 1, 1 << 20, 0)


def _rope_tables():
    rows = SEQ // GRID_W
    row = jnp.repeat(jnp.arange(rows, dtype=F32), GRID_W)
    col = jnp.tile(jnp.arange(GRID_W, dtype=F32), rows)
    n_freq = HD // 4
    inv_freq = ROPE_THETA ** (-jnp.arange(n_freq, dtype=F32) / n_freq)
    ar = row[:, None] * inv_freq
    ac = col[:, None] * inv_freq
    cos = jnp.concatenate([jnp.cos(ar), jnp.cos(ar), jnp.cos(ac), jnp.cos(ac)], axis=-1)
    sin = jnp.concatenate([-jnp.sin(ar), jnp.sin(ar), -jnp.sin(ac), jnp.sin(ac)], axis=-1)
    return cos, sin


def _headnorm_kernel(x_ref, g_ref, *rest, n_heads, scale, rope):
    if rope:
        cos_ref, sin_ref, o_ref = rest
        cos = cos_ref[...]
        sin = sin_ref[...]
        lane = jax.lax.broadcasted_iota(jnp.int32, cos.shape, 1)
        first_half = (lane % (HD // 2)) < (HD // 4)
    else:
        (o_ref,) = rest
    g = g_ref[...] * scale
    for h in range(n_heads):
        x = x_ref[:, h * HD:(h + 1) * HD].astype(F32)
        y = _rms(x, g)
        if rope:
            partner = jnp.where(first_half,
                                pltpu.roll(y, HD - HD // 4, axis=1),
                                pltpu.roll(y, HD // 4, axis=1))
            y = y * cos + partner * sin
        o_ref[:, h * HD:(h + 1) * HD] = y.astype(o_ref.dtype)


def _headnorm(src, col_blk, n_heads, g, scale, rope_tabs):
    t = src.shape[0]
    tm = 1024
    w = n_heads * HD
    rope = rope_tabs is not None
    in_specs = [
        pl.BlockSpec((tm, w), lambda i: (i, col_blk)),
        pl.BlockSpec((1, HD), lambda i: (0, 0)),
    ]
    args = [src, g.reshape(1, HD)]
    if rope:
        nt = SEQ // tm
        in_specs += [pl.BlockSpec((tm, HD), lambda i: (i % nt, 0))] * 2
        args += list(rope_tabs)
    return pl.pallas_call(
        functools.partial(_headnorm_kernel, n_heads=n_heads, scale=scale, rope=rope),
        out_shape=jax.ShapeDtypeStruct((t, w), BF16),
        grid=(t // tm,),
        in_specs=in_specs,
        out_specs=pl.BlockSpec((tm, w), lambda i: (i, 0)),
        compiler_params=_cp(("arbitrary",), 44),
        name="head_norm",
    )(*args)


TQ_SUB = 256


def _attn_kernel(q_ref, k_ref, vt_ref, o_ref):
    k = k_ref[...]
    vt = vt_ref[...]
    for r0 in range(0, q_ref.shape[0], TQ_SUB):
        for g0 in range(0, GROUP, 2):
            q = jnp.concatenate([q_ref[r0:r0 + TQ_SUB, g0 * HD:(g0 + 1) * HD],
                                 q_ref[r0:r0 + TQ_SUB, (g0 + 1) * HD:(g0 + 2) * HD]], axis=0)
            st = jax.lax.dot_general(k, q, (((1,), (1,)), ((), ())),
                                     preferred_element_type=F32)
            m = jnp.max(st, axis=0, keepdims=True)
            p = jnp.exp2(st - m)
            l = jnp.sum(p, axis=0, keepdims=True)
            ot = jnp.dot(vt, p.astype(BF16), preferred_element_type=F32)
            ot = ot * (1.0 / l)
            o_ref[r0:r0 + TQ_SUB, g0 * HD:(g0 + 1) * HD] = ot[:, :TQ_SUB].T.astype(o_ref.dtype)
            o_ref[r0:r0 + TQ_SUB, (g0 + 1) * HD:(g0 + 2) * HD] = ot[:, TQ_SUB:].T.astype(o_ref.dtype)


def _attention(q, k, vt, lq):
    s_len = k.shape[1]
    tq = min(lq, 1024)
    nq = lq // tq
    return pl.pallas_call(
        _attn_kernel,
        out_shape=jax.ShapeDtypeStruct((BATCH * lq, D), BF16),
        grid=(BATCH, N_KV, nq),
        in_specs=[
            pl.BlockSpec((tq, GROUP * HD), lambda b, h, i: (b * nq + i, h)),
            pl.BlockSpec((None, s_len, HD), lambda b, h, i: (b, 0, h)),
            pl.BlockSpec((None, HD, s_len), lambda b, h, i: (b, h, 0)),
        ],
        out_specs=pl.BlockSpec((tq, GROUP * HD), lambda b, h, i: (b * nq + i, h)),
        compiler_params=_cp(("arbitrary", "arbitrary", "arbitrary"), 48),
        name="attention",
    )(q, k, vt)


def _gmlp_kernel(u_ref, v_ref, gv_ref, ws_ref, bs_ref, o_ref, *, tm):
    v = _rms(v_ref[...].astype(F32), gv_ref[...]).astype(BF16)
    for g in range(GG):
        c0 = g * CHUNK
        w = ws_ref[g].astype(BF16)
        for n in range(tm // CHUNK):
            r0 = n * CHUNK
            mixed = jnp.dot(w, v[r0:r0 + CHUNK, c0:c0 + CHUNK],
                            preferred_element_type=F32) + bs_ref[g]
            u = u_ref[r0:r0 + CHUNK, c0:c0 + CHUNK].astype(F32)
            o_ref[r0:r0 + CHUNK, c0:c0 + CHUNK] = (u * mixed).astype(o_ref.dtype)


def _gmlp(zg, gv, ws, bs):
    t = zg.shape[0]
    tm = 512
    return pl.pallas_call(
        functools.partial(_gmlp_kernel, tm=tm),
        out_shape=jax.ShapeDtypeStruct((t, GW), BF16),
        grid=(t // tm,),
        in_specs=[
            pl.BlockSpec((tm, GW), lambda i: (i, 0)),
            pl.BlockSpec((tm, GW), lambda i: (i, 1)),
            pl.BlockSpec((1, GW), lambda i: (0, 0)),
            pl.BlockSpec((GG, CHUNK, CHUNK), lambda i: (0, 0, 0)),
            pl.BlockSpec((GG, CHUNK, CHUNK), lambda i: (0, 0, 0)),
        ],
        out_specs=pl.BlockSpec((tm, GW), lambda i: (i, 0)),
        compiler_params=_cp(("arbitrary",), 32),
        name="gmlp",
    )(zg, zg, gv.reshape(1, GW), ws, bs)


def _dwconv3(x, w, seq_len):
    rows = x.shape[0]
    pos = jax.lax.broadcasted_iota(jnp.int32, x.shape, 0) % seq_len
    prev = jnp.where(pos == 0, 0.0, pltpu.roll(x, 1, axis=0))
    nxt = jnp.where(pos == seq_len - 1, 0.0, pltpu.roll(x, rows - 1, axis=0))
    return prev * w[0:1, :] + x * w[1:2, :] + nxt * w[2:3, :]


def _sconv_kernel(b_ref, c_ref, h_ref, w_ref, o_ref, *, seq_len):
    t = c_ref[...].astype(F32) * h_ref[...].astype(F32)
    y = b_ref[...].astype(F32) * _dwconv3(t, w_ref[...], seq_len)
    o_ref[...] = y.astype(o_ref.dtype)


def _sconv(raw, w, seq_len, rows):
    t = raw.shape[0]
    tc = 256
    return pl.pallas_call(
        functools.partial(_sconv_kernel, seq_len=seq_len),
        out_shape=jax.ShapeDtypeStruct((t, CW), BF16),
        grid=(t // rows, CW // tc),
        in_specs=[
            pl.BlockSpec((rows, tc), lambda i, j: (i, R_B // tc + j)),
            pl.BlockSpec((rows, tc), lambda i, j: (i, R_C // tc + j)),
            pl.BlockSpec((rows, tc), lambda i, j: (i, R_H // tc + j)),
            pl.BlockSpec((3, tc), lambda i, j: (0, j)),
        ],
        out_specs=pl.BlockSpec((rows, tc), lambda i, j: (i, j)),
        compiler_params=_cp(("arbitrary", "arbitrary"), 40),
        name="short_conv",
    )(raw, raw, raw, w)


def _merge_kernel(ya_ref, yb_ref, at_ref, wa_ref, wb_ref, wc_ref,
                  ga_ref, gb_ref, gc_ref, o_ref, wa_s, wb_s, wc_s):
    @pl.when(pl.program_id(1) == 0)
    def _():
        wa_s[...] = wa_ref[...].astype(BF16)
        wb_s[...] = wb_ref[...].astype(BF16)
        wc_s[...] = wc_ref[...].astype(BF16)

    a = jnp.dot(ya_ref[...], wa_s[...], preferred_element_type=F32)
    m = ga_ref[...].astype(F32) * a
    b = jnp.dot(yb_ref[...], wb_s[...], preferred_element_type=F32)
    m = m + gb_ref[...].astype(F32) * b
    c = jnp.dot(at_ref[...], wc_s[...], preferred_element_type=F32)
    m = m + gc_ref[...].astype(F32) * c
    o_ref[...] = m.astype(o_ref.dtype)


def _merge(ya, yb, attn, gates, w_pa, w_pb, w_pc, layer):
    t = ya.shape[0]
    tm = 1024
    tn = 512
    nb = D // tn
    return pl.pallas_call(
        _merge_kernel,
        out_shape=jax.ShapeDtypeStruct((t, D), BF16),
        grid=(nb, t // tm),
        in_specs=[
            pl.BlockSpec((tm, GW), lambda j, i: (i, 0)),
            pl.BlockSpec((tm, CW), lambda j, i: (i, 0)),
            pl.BlockSpec((tm, D), lambda j, i: (i, 0)),
            pl.BlockSpec((None, GW, tn), lambda j, i: (layer, 0, j)),
            pl.BlockSpec((None, CW, tn), lambda j, i: (layer, 0, j)),
            pl.BlockSpec((None, D, tn), lambda j, i: (layer, 0, j)),
            pl.BlockSpec((tm, tn), lambda j, i: (i, j)),
            pl.BlockSpec((tm, tn), lambda j, i: (i, nb + j)),
            pl.BlockSpec((tm, tn), lambda j, i: (i, 2 * nb + j)),
        ],
        out_specs=pl.BlockSpec((tm, tn), lambda j, i: (i, j)),
        scratch_shapes=[pltpu.VMEM((GW, tn), BF16), pltpu.VMEM((CW, tn), BF16),
                        pltpu.VMEM((D, tn), BF16)],
        compiler_params=_cp(("arbitrary", "arbitrary"), 56),
        name="merge",
    )(ya, yb, attn, w_pa, w_pb, w_pc, gates, gates, gates)


def _outproj_kernel(m_ref, w_ref, x_ref, gt_ref, g_ref, sh_ref, sc_ref, x1_ref, h_ref, w_s):
    @pl.when(pl.program_id(0) == 0)
    def _():
        w_s[...] = w_ref[...].astype(BF16)

    y = jnp.dot(m_ref[...], w_s[...], preferred_element_type=F32)
    x1 = x_ref[...] + gt_ref[...] * y
    x1_ref[...] = x1
    h_ref[...] = (_rms(x1, g_ref[...]) * (1.0 + sc_ref[...]) + sh_ref[...]).astype(h_ref.dtype)


def _outproj(m, w_o, x, g_ffn, mods5, layer, ctx):
    t = m.shape[0]
    tm = 512
    row_of = _row_of(ctx, SEQ // tm, 0)
    return pl.pallas_call(
        _outproj_kernel,
        out_shape=(jax.ShapeDtypeStruct((t, D), F32), jax.ShapeDtypeStruct((t, D), BF16)),
        grid=(t // tm,),
        in_specs=[
            pl.BlockSpec((tm, D), lambda i: (i, 0)),
            pl.BlockSpec((None, D, D), lambda i: (layer, 0, 0), pipeline_mode=pl.Buffered(1)),
            pl.BlockSpec((tm, D), lambda i: (i, 0)),
            _mod_spec(layer, 2, row_of),
            pl.BlockSpec((1, D), lambda i: (0, 0)),
            _mod_spec(layer, 3, row_of),
            _mod_spec(layer, 4, row_of),
        ],
        out_specs=(pl.BlockSpec((tm, D), lambda i: (i, 0)),
                   pl.BlockSpec((tm, D), lambda i: (i, 0))),
        scratch_shapes=[pltpu.VMEM((D, D), BF16)],
        compiler_params=_cp(("arbitrary",), 56),
        name="out_proj",
    )(m, w_o, x, mods5, g_ffn.reshape(1, D), mods5, mods5)


FF_BLOCKS = D_FF // LANES
FF_SUBW = 2 * LANES
FF_NSUB = 2
TN_FF = FF_NSUB * FF_SUBW
FF_NBLK = TN_FF // LANES


def _ffn_up_kernel(h_ref, *refs, seq_len):
    a_refs = refs[0:FF_NBLK]
    b_refs = refs[FF_NBLK:2 * FF_NBLK]
    ca_refs = refs[2 * FF_NBLK:3 * FF_NBLK]
    cb_refs = refs[3 * FF_NBLK:4 * FF_NBLK]
    o_ref = refs[4 * FF_NBLK]
    h = h_ref[...]
    for s in range(FF_NSUB):
        k0 = s * (FF_SUBW // LANES)
        k1 = k0 + FF_SUBW // LANES
        wa = jnp.concatenate([r[...] for r in a_refs[k0:k1]], axis=1).astype(BF16)
        wb = jnp.concatenate([r[...] for r in b_refs[k0:k1]], axis=1).astype(BF16)
        ca = jnp.concatenate([r[...] for r in ca_refs[k0:k1]], axis=1)
        cb = jnp.concatenate([r[...] for r in cb_refs[k0:k1]], axis=1)
        a = _dwconv3(jnp.dot(h, wa, preferred_element_type=F32), ca, seq_len)
        b = _dwconv3(jnp.dot(h, wb, preferred_element_type=F32), cb, seq_len)
        o_ref[:, s * FF_SUBW:(s + 1) * FF_SUBW] = (a * _sigmoid(a) * b).astype(o_ref.dtype)


def _ffn_up(h, w_up, w_conv, layer, seq_len, rows):
    t = h.shape[0]
    nj = pl.cdiv(D_FF, TN_FF)
    last = 2 * FF_BLOCKS - 1

    def wspec(rows_, off, k):
        return pl.BlockSpec((None, rows_, LANES),
                            lambda i, j: (layer, 0, jnp.minimum(off + FF_NBLK * j + k, last)))

    blocks = range(FF_NBLK)
    specs = ([wspec(D, 0, k) for k in blocks] + [wspec(D, FF_BLOCKS, k) for k in blocks]
             + [wspec(3, 0, k) for k in blocks] + [wspec(3, FF_BLOCKS, k) for k in blocks])
    return pl.pallas_call(
        functools.partial(_ffn_up_kernel, seq_len=seq_len),
        out_shape=jax.ShapeDtypeStruct((t, D_FF), BF16),
        grid=(t // rows, nj),
        in_specs=[pl.BlockSpec((rows, D), lambda i, j: (i, 0))] + specs,
        out_specs=pl.BlockSpec((rows, TN_FF), lambda i, j: (i, j)),
        compiler_params=_cp(("arbitrary", "arbitrary"), 56),
        name="ffn_up",
    )(h, *([w_up] * (2 * FF_NBLK)), *([w_conv] * (2 * FF_NBLK)))


def _ffn_down_kernel(g_ref, w_ref, x_ref, gt_ref, o_ref, w_s):
    @pl.when(pl.program_id(1) == 0)
    def _():
        w_s[...] = w_ref[...].astype(BF16)

    y = jnp.dot(g_ref[...], w_s[...], preferred_element_type=F32)
    o_ref[...] = x_ref[...] + gt_ref[...] * y


def _ffn_down(g, w_down, x, mods5, layer, ctx):
    t = g.shape[0]
    tm = 512
    tn = 512
    row_of = _row_of(ctx, SEQ // tm, 1)
    return pl.pallas_call(
        _ffn_down_kernel,
        out_shape=jax.ShapeDtypeStruct((t, D), F32),
        grid=(D // tn, t // tm),
        in_specs=[
            pl.BlockSpec((tm, D_FF), lambda j, i: (i, 0)),
            pl.BlockSpec((None, D_FF, tn), lambda j, i: (layer, 0, j)),
            pl.BlockSpec((tm, tn), lambda j, i: (i, j)),
            _mod_spec(layer, 5, row_of, tn, lambda j, i: j),
        ],
        out_specs=pl.BlockSpec((tm, tn), lambda j, i: (i, j)),
        scratch_shapes=[pltpu.VMEM((D_FF, tn), BF16)],
        compiler_params=_cp(("arbitrary", "arbitrary"), 54),
        name="ffn_down",
    )(g, w_down, x, mods5)


def kernel(x, c, ctx, c_ctx, w_ada, b_ada, g_mix, w_in, g_q, g_k, w_gmlp, b_gmlp, g_gmlp_v,
           w_sconv, w_pa, w_pb, w_pc, w_o, g_ffn, w_up, w_ffn_conv, w_down, g_final):
    cvec = jnp.concatenate([c, c_ctx[None, :], jnp.zeros((3, D), F32)], axis=0)
    mods5 = _mods(cvec, w_ada, b_ada).reshape(DEPTH, 8, 6, 1, D)
    rope_tabs = _rope_tables()
    q_scale = HD ** -0.5 * math.log2(math.e)

    xs = x.reshape(BATCH * SEQ, D)
    cs = ctx.reshape(BATCH * CTX, D)

    for l in range(DEPTH):
        last = l == DEPTH - 1
        bs = jnp.broadcast_to(b_gmlp[l][:, :, None], (GG, CHUNK, CHUNK))
        hx = _normmod(xs, g_mix[l], mods5, l, False)
        hc = _normmod(cs, g_mix[l], mods5, l, True)

        raw_x = _inproj(hx, w_in, l, COLS_RAW, "raw")
        zg_x = _inproj(hx, w_in, l, COLS_GELU, "gelu")
        gates_x = _inproj(hx, w_in, l, COLS_GATE, "sigmoid")
        if not last:
            raw_c = _inproj(hc, w_in, l, COLS_RAW, "raw")
            kc = _headnorm(raw_c, R_K // KV_DIM, N_KV, g_k[l], 1.0, None)
            vc = raw_c[:, R_V:R_V + KV_DIM]
        else:
            kv_c = _inproj(hc, w_in, l, COLS_KV, "raw")
            kc = _headnorm(kv_c, 0, N_KV, g_k[l], 1.0, None)
            vc = kv_c[:, KV_DIM:]
        qx = _headnorm(raw_x, 0, N_HEADS, g_q[l], q_scale, rope_tabs)
        kx = _headnorm(raw_x, R_K // KV_DIM, N_KV, g_k[l], 1.0, rope_tabs)
        vx = raw_x[:, R_V:R_V + KV_DIM]
        kc3 = kc.reshape(BATCH, CTX, KV_DIM)
        vc3 = vc.reshape(BATCH, CTX, KV_DIM)
        k_all = jnp.concatenate([kx.reshape(BATCH, SEQ, KV_DIM), kc3], axis=1)
        vt_all = jnp.swapaxes(
            jnp.concatenate([vx.reshape(BATCH, SEQ, KV_DIM), vc3], axis=1), 1, 2)
        attn_x = _attention(qx, k_all, vt_all, SEQ)
        ya = _gmlp(zg_x, g_gmlp_v[l], w_gmlp[l], bs)
        yb = _sconv(raw_x, w_sconv[l], SEQ, SEQ)
        m = _merge(ya, yb, attn_x, gates_x, w_pa, w_pb, w_pc, l)
        x1, fx = _outproj(m, w_o, xs, g_ffn[l], mods5, l, False)

        gx = _ffn_up(fx, w_up, w_ffn_conv, l, SEQ, SEQ)
        xs = _ffn_down(gx, w_down, x1, mods5, l, False)

        if not last:
            zg_c = _inproj(hc, w_in, l, COLS_GELU, "gelu")
            gates_c = _inproj(hc, w_in, l, COLS_GATE, "sigmoid")
            qc = _headnorm(raw_c, 0, N_HEADS, g_q[l], q_scale, None)
            attn_c = _attention(qc, kc3, jnp.swapaxes(vc3, 1, 2), CTX)
            yac = _gmlp(zg_c, g_gmlp_v[l], w_gmlp[l], bs)
            ybc = _sconv(raw_c, w_sconv[l], CTX, BATCH * CTX)
            mc = _merge(yac, ybc, attn_c, gates_c, w_pa, w_pb, w_pc, l)
            c1, fc = _outproj(mc, w_o, cs, g_ffn[l], mods5, l, True)
            gc = _ffn_up(fc, w_up, w_ffn_conv, l, CTX, BATCH * CTX)
            cs = _ffn_down(gc, w_down, c1, mods5, l, True)

    return _final_norm(xs, g_final).reshape(BATCH, SEQ, D)
```

```python
import functools
import math

import jax
import jax.numpy as jnp
from jax.experimental import pallas as pl
from jax.experimental.pallas import tpu as pltpu

F32 = jnp.float32
BF16 = jnp.bfloat16

D = 2048
BATCH = 4
SEQ = 2048
CTX = 256
DEPTH = 2
GRID_W = 64
N_HEADS = 16
N_KV = 4
HD = 128
GROUP = N_HEADS // N_KV
CHUNK = 128
GW = D // 2
GG = 8
CW = D // 2
D_FF = 5504
EPS = 1e-6
ROPE_THETA = 10000.0
KV_DIM = N_KV * HD
IN_DIM = D + 2 * KV_DIM + 2 * GW + 3 * CW + 3 * D
OFF_K = D
OFF_U = D + 2 * KV_DIM
OFF_B = OFF_U + 2 * GW
OFF_G = OFF_B + 3 * CW
R_K = D
R_V = D + KV_DIM
R_B = D + 2 * KV_DIM
R_C = R_B + CW
R_H = R_C + CW
LANES = 128

VMEM_PHYS_V7X = 64 * 1024 * 1024
VMEM_CAP = VMEM_PHYS_V7X - 8 * 1024 * 1024


def _cp(sem, vmem_mb):
    return pltpu.CompilerParams(
        dimension_semantics=sem,
        vmem_limit_bytes=min(int(vmem_mb * 1024 * 1024), VMEM_CAP))


def _rms(xf, g):
    ms = jnp.mean(xf * xf, axis=-1, keepdims=True)
    return xf * jax.lax.rsqrt(ms + EPS) * g


def _sigmoid(x):
    return 0.5 * jnp.tanh(0.5 * x) + 0.5


def _gelu_tanh(x):
    c = math.sqrt(2.0 / math.pi)
    return 0.5 * x * (1.0 + jnp.tanh(c * (x + 0.044715 * (x * x * x))))


def _mods_kernel(c_ref, w_ref, b_ref, o_ref):
    c = c_ref[...]
    s = (c * _sigmoid(c)).astype(BF16)
    w = w_ref[...].astype(BF16)
    o_ref[...] = jnp.dot(s, w, preferred_element_type=F32) + b_ref[...]


def _mods(cvec, w_ada, b_ada):
    tn = 1024
    nj = 6 * D // tn
    return pl.pallas_call(
        _mods_kernel,
        out_shape=jax.ShapeDtypeStruct((DEPTH, 8, 6 * D), F32),
        grid=(DEPTH, nj),
        in_specs=[
            pl.BlockSpec((8, D), lambda l, j: (0, 0)),
            pl.BlockSpec((None, D, tn), lambda l, j: (l, 0, j)),
            pl.BlockSpec((None, 1, tn), lambda l, j: (l, 0, j)),
        ],
        out_specs=pl.BlockSpec((None, 8, tn), lambda l, j: (l, 0, j)),
        compiler_params=_cp(("arbitrary", "arbitrary"), 40),
        name="adaln_mods",
    )(cvec, w_ada, b_ada.reshape(DEPTH, 1, 6 * D))


def _mod_spec(layer, k, row_of, width=D, col_of=None):
    if col_of is None:
        return pl.BlockSpec((None, None, None, 1, width),
                            lambda *g: (layer, row_of(*g), k, 0, 0))
    return pl.BlockSpec((None, None, None, 1, width),
                        lambda *g: (layer, row_of(*g), k, 0, col_of(*g)))


def _row_of(ctx, tiles_per_batch, axis):
    if ctx:
        return lambda *g: BATCH
    return lambda *g: g[axis] // tiles_per_batch


def _normmod_kernel(x_ref, g_ref, sh_ref, sc_ref, o_ref):
    y = _rms(x_ref[...], g_ref[...])
    o_ref[...] = (y * (1.0 + sc_ref[...]) + sh_ref[...]).astype(o_ref.dtype)


def _norm_kernel(x_ref, g_ref, o_ref):
    o_ref[...] = _rms(x_ref[...], g_ref[...]).astype(o_ref.dtype)


def _normmod(x, g, mods5, layer, ctx):
    t = x.shape[0]
    tm = 1024
    row_of = _row_of(ctx, SEQ // tm, 0)
    return pl.pallas_call(
        _normmod_kernel,
        out_shape=jax.ShapeDtypeStruct((t, D), BF16),
        grid=(t // tm,),
        in_specs=[
            pl.BlockSpec((tm, D), lambda i: (i, 0)),
            pl.BlockSpec((1, D), lambda i: (0, 0)),
            _mod_spec(layer, 0, row_of),
            _mod_spec(layer, 1, row_of),
        ],
        out_specs=pl.BlockSpec((tm, D), lambda i: (i, 0)),
        compiler_params=_cp(("arbitrary",), 44),
        name="norm_mod",
    )(x, g.reshape(1, D), mods5, mods5)


def _final_norm(x, g):
    t = x.shape[0]
    tm = 1024
    return pl.pallas_call(
        _norm_kernel,
        out_shape=jax.ShapeDtypeStruct((t, D), F32),
        grid=(t // tm,),
        in_specs=[
            pl.BlockSpec((tm, D), lambda i: (i, 0)),
            pl.BlockSpec((1, D), lambda i: (0, 0)),
        ],
        out_specs=pl.BlockSpec((tm, D), lambda i: (i, 0)),
        compiler_params=_cp(("arbitrary",), 48),
        name="final_norm",
    )(x, g.reshape(1, D))


TN_IN = 1024
_ACTS = {"raw": lambda a: a, "gelu": _gelu_tanh, "sigmoid": _sigmoid}


def _inproj_kernel(h_ref, w_ref, o_ref, wb_ref, *, act):
    @pl.when(pl.program_id(1) == 0)
    def _():
        wb_ref[...] = w_ref[...].astype(BF16)

    acc = jnp.dot(h_ref[...], wb_ref[...], preferred_element_type=F32)
    o_ref[...] = _ACTS[act](acc).astype(o_ref.dtype)


def _inproj(h, w, layer, col_tiles, act):
    t = h.shape[0]
    tm = 1024
    first, nj, skip_at, skip = col_tiles

    def wmap(j, i):
        return (layer, 0, first + j + skip * (j // skip_at))

    return pl.pallas_call(
        functools.partial(_inproj_kernel, act=act),
        out_shape=jax.ShapeDtypeStruct((t, nj * TN_IN), BF16),
        grid=(nj, t // tm),
        in_specs=[
            pl.BlockSpec((tm, D), lambda j, i: (i, 0)),
            pl.BlockSpec((None, D, TN_IN), wmap),
        ],
        out_specs=pl.BlockSpec((tm, TN_IN), lambda j, i: (i, j)),
        scratch_shapes=[pltpu.VMEM((D, TN_IN), BF16)],
        compiler_params=_cp(("arbitrary", "arbitrary"), 48),
        name="in_proj_" + act,
    )(h, w)


COLS_RAW = (0, 6, 3, (OFF_B - OFF_U) // TN_IN)
COLS_GELU = (OFF_U // TN_IN, (OFF_B - OFF_U) // TN_IN, 1 << 20, 0)
COLS_GATE = (OFF_G // TN_IN, 3 * D // TN_IN, 1 << 20, 0)
COLS_KV = (OFF_K // TN_IN, 1, 1 << 20, 0)


def _rope_tables():
    rows = SEQ // GRID_W
    row = jnp.repeat(jnp.arange(rows, dtype=F32), GRID_W)
    col = jnp.tile(jnp.arange(GRID_W, dtype=F32), rows)
    n_freq = HD // 4
    inv_freq = ROPE_THETA ** (-jnp.arange(n_freq, dtype=F32) / n_freq)
    ar = row[:, None] * inv_freq
    ac = col[:, None] * inv_freq
    cos = jnp.concatenate([jnp.cos(ar), jnp.cos(ar), jnp.cos(ac), jnp.cos(ac)], axis=-1)
    sin = jnp.concatenate([-jnp.sin(ar), jnp.sin(ar), -jnp.sin(ac), jnp.sin(ac)], axis=-1)
    return cos, sin


def _headnorm_kernel(x_ref, g_ref, *rest, n_heads, scale, rope):
    if rope:
        cos_ref, sin_ref, o_ref = rest
        cos = cos_ref[...]
        sin = sin_ref[...]
        lane = jax.lax.broadcasted_iota(jnp.int32, cos.shape, 1)
        first_half = (lane % (HD // 2)) < (HD // 4)
    else:
        (o_ref,) = rest
    g = g_ref[...] * scale
    for h in range(n_heads):
        x = x_ref[:, h * HD:(h + 1) * HD].astype(F32)
        y = _rms(x, g)
        if rope:
            partner = jnp.where(first_half,
                                pltpu.roll(y, HD - HD // 4, axis=1),
                                pltpu.roll(y, HD // 4, axis=1))
            y = y * cos + partner * sin
        o_ref[:, h * HD:(h + 1) * HD] = y.astype(o_ref.dtype)


def _headnorm(src, col_blk, n_heads, g, scale, rope_tabs):
    t = src.shape[0]
    tm = 1024
    w = n_heads * HD
    rope = rope_tabs is not None
    in_specs = [
        pl.BlockSpec((tm, w), lambda i: (i, col_blk)),
        pl.BlockSpec((1, HD), lambda i: (0, 0)),
    ]
    args = [src, g.reshape(1, HD)]
    if rope:
        nt = SEQ // tm
        in_specs += [pl.BlockSpec((tm, HD), lambda i: (i % nt, 0))] * 2
        args += list(rope_tabs)
    return pl.pallas_call(
        functools.partial(_headnorm_kernel, n_heads=n_heads, scale=scale, rope=rope),
        out_shape=jax.ShapeDtypeStruct((t, w), BF16),
        grid=(t // tm,),
        in_specs=in_specs,
        out_specs=pl.BlockSpec((tm, w), lambda i: (i, 0)),
        compiler_params=_cp(("arbitrary",), 44),
        name="head_norm",
    )(*args)


TQ_SUB = 256


def _attn_kernel(q_ref, k_ref, vt_ref, o_ref):
    k = k_ref[...]
    vt = vt_ref[...]
    units = [(r0, g0) for r0 in range(0, q_ref.shape[0], TQ_SUB) for g0 in range(0, GROUP, 2)]

    def scores(u):
        r0, g0 = units[u]
        q = jnp.concatenate([q_ref[r0:r0 + TQ_SUB, g0 * HD:(g0 + 1) * HD],
                             q_ref[r0:r0 + TQ_SUB, (g0 + 1) * HD:(g0 + 2) * HD]], axis=0)
        return jax.lax.dot_general(k, q, (((1,), (1,)), ((), ())),
                                   preferred_element_type=F32)

    st_next = scores(0)
    for u, (r0, g0) in enumerate(units):
        st = st_next
        if u + 1 < len(units):
            st_next = scores(u + 1)
        m = jnp.max(st, axis=0, keepdims=True)
        p = jnp.exp2(st - m)
        l = jnp.sum(p, axis=0, keepdims=True)
        ot = jnp.dot(vt, p.astype(BF16), preferred_element_type=F32)
        ot = ot * (1.0 / l)
        o_ref[r0:r0 + TQ_SUB, g0 * HD:(g0 + 1) * HD] = ot[:, :TQ_SUB].T.astype(o_ref.dtype)
        o_ref[r0:r0 + TQ_SUB, (g0 + 1) * HD:(g0 + 2) * HD] = ot[:, TQ_SUB:].T.astype(o_ref.dtype)


def _attention(q, k, vt, lq):
    s_len = k.shape[1]
    tq = min(lq, 1024)
    nq = lq // tq
    return pl.pallas_call(
        _attn_kernel,
        out_shape=jax.ShapeDtypeStruct((BATCH * lq, D), BF16),
        grid=(BATCH, N_KV, nq),
        in_specs=[
            pl.BlockSpec((tq, GROUP * HD), lambda b, h, i: (b * nq + i, h)),
            pl.BlockSpec((None, s_len, HD), lambda b, h, i: (b, 0, h)),
            pl.BlockSpec((None, HD, s_len), lambda b, h, i: (b, h, 0)),
        ],
        out_specs=pl.BlockSpec((tq, GROUP * HD), lambda b, h, i: (b * nq + i, h)),
        compiler_params=_cp(("arbitrary", "arbitrary", "arbitrary"), 48),
        name="attention",
    )(q, k, vt)


def _gmlp_kernel(u_ref, v_ref, gv_ref, ws_ref, bs_ref, o_ref, *, tm):
    v = _rms(v_ref[...].astype(F32), gv_ref[...]).astype(BF16)
    for g in range(GG):
        c0 = g * CHUNK
        w = ws_ref[g].astype(BF16)
        for n in range(tm // CHUNK):
            r0 = n * CHUNK
            mixed = jnp.dot(w, v[r0:r0 + CHUNK, c0:c0 + CHUNK],
                            preferred_element_type=F32) + bs_ref[g]
            u = u_ref[r0:r0 + CHUNK, c0:c0 + CHUNK].astype(F32)
            o_ref[r0:r0 + CHUNK, c0:c0 + CHUNK] = (u * mixed).astype(o_ref.dtype)


def _gmlp(zg, gv, ws, bs):
    t = zg.shape[0]
    tm = 512
    return pl.pallas_call(
        functools.partial(_gmlp_kernel, tm=tm),
        out_shape=jax.ShapeDtypeStruct((t, GW), BF16),
        grid=(t // tm,),
        in_specs=[
            pl.BlockSpec((tm, GW), lambda i: (i, 0)),
            pl.BlockSpec((tm, GW), lambda i: (i, 1)),
            pl.BlockSpec((1, GW), lambda i: (0, 0)),
            pl.BlockSpec((GG, CHUNK, CHUNK), lambda i: (0, 0, 0)),
            pl.BlockSpec((GG, CHUNK, CHUNK), lambda i: (0, 0, 0)),
        ],
        out_specs=pl.BlockSpec((tm, GW), lambda i: (i, 0)),
        compiler_params=_cp(("arbitrary",), 32),
        name="gmlp",
    )(zg, zg, gv.reshape(1, GW), ws, bs)


def _dwconv3(x, w, seq_len):
    rows = x.shape[0]
    pos = jax.lax.broadcasted_iota(jnp.int32, x.shape, 0) % seq_len
    prev = jnp.where(pos == 0, 0.0, pltpu.roll(x, 1, axis=0))
    nxt = jnp.where(pos == seq_len - 1, 0.0, pltpu.roll(x, rows - 1, axis=0))
    return prev * w[0:1, :] + x * w[1:2, :] + nxt * w[2:3, :]


def _sconv_kernel(b_ref, c_ref, h_ref, w_ref, o_ref, *, seq_len):
    t = c_ref[...].astype(F32) * h_ref[...].astype(F32)
    y = b_ref[...].astype(F32) * _dwconv3(t, w_ref[...], seq_len)
    o_ref[...] = y.astype(o_ref.dtype)


def _sconv(raw, w, seq_len, rows):
    t = raw.shape[0]
    tc = 256
    return pl.pallas_call(
        functools.partial(_sconv_kernel, seq_len=seq_len),
        out_shape=jax.ShapeDtypeStruct((t, CW), BF16),
        grid=(t // rows, CW // tc),
        in_specs=[
            pl.BlockSpec((rows, tc), lambda i, j: (i, R_B // tc + j)),
            pl.BlockSpec((rows, tc), lambda i, j: (i, R_C // tc + j)),
            pl.BlockSpec((rows, tc), lambda i, j: (i, R_H // tc + j)),
            pl.BlockSpec((3, tc), lambda i, j: (0, j)),
        ],
        out_specs=pl.BlockSpec((rows, tc), lambda i, j: (i, j)),
        compiler_params=_cp(("arbitrary", "arbitrary"), 40),
        name="short_conv",
    )(raw, raw, raw, w)


def _merge_kernel(ya_ref, yb_ref, at_ref, wa_ref, wb_ref, wc_ref,
                  ga_ref, gb_ref, gc_ref, o_ref, wa_s, wb_s, wc_s):
    @pl.when(pl.program_id(1) == 0)
    def _():
        wa_s[...] = wa_ref[...].astype(BF16)
        wb_s[...] = wb_ref[...].astype(BF16)
        wc_s[...] = wc_ref[...].astype(BF16)

    a = jnp.dot(ya_ref[...], wa_s[...], preferred_element_type=F32)
    m = ga_ref[...].astype(F32) * a
    b = jnp.dot(yb_ref[...], wb_s[...], preferred_element_type=F32)
    m = m + gb_ref[...].astype(F32) * b
    c = jnp.dot(at_ref[...], wc_s[...], preferred_element_type=F32)
    m = m + gc_ref[...].astype(F32) * c
    o_ref[...] = m.astype(o_ref.dtype)


def _merge(ya, yb, attn, gates, w_pa, w_pb, w_pc, layer):
    t = ya.shape[0]
    tm = 1024
    tn = 512
    nb = D // tn
    return pl.pallas_call(
        _merge_kernel,
        out_shape=jax.ShapeDtypeStruct((t, D), BF16),
        grid=(nb, t // tm),
        in_specs=[
            pl.BlockSpec((tm, GW), lambda j, i: (i, 0)),
            pl.BlockSpec((tm, CW), lambda j, i: (i, 0)),
            pl.BlockSpec((tm, D), lambda j, i: (i, 0)),
            pl.BlockSpec((None, GW, tn), lambda j, i: (layer, 0, j)),
            pl.BlockSpec((None, CW, tn), lambda j, i: (layer, 0, j)),
            pl.BlockSpec((None, D, tn), lambda j, i: (layer, 0, j)),
            pl.BlockSpec((tm, tn), lambda j, i: (i, j)),
            pl.BlockSpec((tm, tn), lambda j, i: (i, nb + j)),
            pl.BlockSpec((tm, tn), lambda j, i: (i, 2 * nb + j)),
        ],
        out_specs=pl.BlockSpec((tm, tn), lambda j, i: (i, j)),
        scratch_shapes=[pltpu.VMEM((GW, tn), BF16), pltpu.VMEM((CW, tn), BF16),
                        pltpu.VMEM((D, tn), BF16)],
        compiler_params=_cp(("arbitrary", "arbitrary"), 56),
        name="merge",
    )(ya, yb, attn, w_pa, w_pb, w_pc, gates, gates, gates)


def _outproj_kernel(m_ref, w_ref, x_ref, gt_ref, g_ref, sh_ref, sc_ref, x1_ref, h_ref, w_s):
    @pl.when(pl.program_id(0) == 0)
    def _():
        w_s[...] = w_ref[...].astype(BF16)

    y = jnp.dot(m_ref[...], w_s[...], preferred_element_type=F32)
    x1 = x_ref[...] + gt_ref[...] * y
    x1_ref[...] = x1
    h_ref[...] = (_rms(x1, g_ref[...]) * (1.0 + sc_ref[...]) + sh_ref[...]).astype(h_ref.dtype)


def _outproj(m, w_o, x, g_ffn, mods5, layer, ctx):
    t = m.shape[0]
    tm = 512
    row_of = _row_of(ctx, SEQ // tm, 0)
    return pl.pallas_call(
        _outproj_kernel,
        out_shape=(jax.ShapeDtypeStruct((t, D), F32), jax.ShapeDtypeStruct((t, D), BF16)),
        grid=(t // tm,),
        in_specs=[
            pl.BlockSpec((tm, D), lambda i: (i, 0)),
            pl.BlockSpec((None, D, D), lambda i: (layer, 0, 0), pipeline_mode=pl.Buffered(1)),
            pl.BlockSpec((tm, D), lambda i: (i, 0)),
            _mod_spec(layer, 2, row_of),
            pl.BlockSpec((1, D), lambda i: (0, 0)),
            _mod_spec(layer, 3, row_of),
            _mod_spec(layer, 4, row_of),
        ],
        out_specs=(pl.BlockSpec((tm, D), lambda i: (i, 0)),
                   pl.BlockSpec((tm, D), lambda i: (i, 0))),
        scratch_shapes=[pltpu.VMEM((D, D), BF16)],
        compiler_params=_cp(("arbitrary",), 56),
        name="out_proj",
    )(m, w_o, x, mods5, g_ffn.reshape(1, D), mods5, mods5)


FF_BLOCKS = D_FF // LANES
FF_SUBW = 2 * LANES
FF_NSUB = 2
TN_FF = FF_NSUB * FF_SUBW
FF_NBLK = TN_FF // LANES
FF_MCHUNK = 256


def _chunked_dot(h_ref, w):
    parts = [jnp.dot(h_ref[r0:r0 + FF_MCHUNK, :], w, preferred_element_type=F32)
             for r0 in range(0, h_ref.shape[0], FF_MCHUNK)]
    return jnp.concatenate(parts, axis=0)


def _ffn_up_kernel(h_ref, *refs, seq_len):
    a_refs = refs[0:FF_NBLK]
    b_refs = refs[FF_NBLK:2 * FF_NBLK]
    ca_refs = refs[2 * FF_NBLK:3 * FF_NBLK]
    cb_refs = refs[3 * FF_NBLK:4 * FF_NBLK]
    o_ref = refs[4 * FF_NBLK]
    ws = []
    for s in range(FF_NSUB):
        k0 = s * (FF_SUBW // LANES)
        k1 = k0 + FF_SUBW // LANES
        ws.append((jnp.concatenate([r[...] for r in a_refs[k0:k1]], axis=1).astype(BF16),
                   jnp.concatenate([r[...] for r in b_refs[k0:k1]], axis=1).astype(BF16),
                   jnp.concatenate([r[...] for r in ca_refs[k0:k1]], axis=1),
                   jnp.concatenate([r[...] for r in cb_refs[k0:k1]], axis=1)))
    for s in range(FF_NSUB):
        wa, wb, ca, cb = ws[s]
        a = _dwconv3(_chunked_dot(h_ref, wa), ca, seq_len)
        b = _dwconv3(_chunked_dot(h_ref, wb), cb, seq_len)
        o_ref[:, s * FF_SUBW:(s + 1) * FF_SUBW] = (a * _sigmoid(a) * b).astype(o_ref.dtype)


def _ffn_up(h, w_up, w_conv, layer, seq_len, rows):
    t = h.shape[0]
    nj = pl.cdiv(D_FF, TN_FF)
    last = 2 * FF_BLOCKS - 1

    def wspec(rows_, off, k):
        return pl.BlockSpec((None, rows_, LANES),
                            lambda i, j: (layer, 0, jnp.minimum(off + FF_NBLK * j + k, last)))

    blocks = range(FF_NBLK)
    specs = ([wspec(D, 0, k) for k in blocks] + [wspec(D, FF_BLOCKS, k) for k in blocks]
             + [wspec(3, 0, k) for k in blocks] + [wspec(3, FF_BLOCKS, k) for k in blocks])
    return pl.pallas_call(
        functools.partial(_ffn_up_kernel, seq_len=seq_len),
        out_shape=jax.ShapeDtypeStruct((t, D_FF), BF16),
        grid=(t // rows, nj),
        in_specs=[pl.BlockSpec((rows, D), lambda i, j: (i, 0))] + specs,
        out_specs=pl.BlockSpec((rows, TN_FF), lambda i, j: (i, j)),
        compiler_params=_cp(("arbitrary", "arbitrary"), 56),
        name="ffn_up",
    )(h, *([w_up] * (2 * FF_NBLK)), *([w_conv] * (2 * FF_NBLK)))


def _ffn_down_kernel(g_ref, w_ref, x_ref, gt_ref, o_ref, w_s):
    @pl.when(pl.program_id(1) == 0)
    def _():
        w_s[...] = w_ref[...].astype(BF16)

    y = jnp.dot(g_ref[...], w_s[...], preferred_element_type=F32)
    o_ref[...] = x_ref[...] + gt_ref[...] * y


def _ffn_down(g, w_down, x, mods5, layer, ctx):
    t = g.shape[0]
    tm = 512
    tn = 512
    row_of = _row_of(ctx, SEQ // tm, 1)
    return pl.pallas_call(
        _ffn_down_kernel,
        out_shape=jax.ShapeDtypeStruct((t, D), F32),
        grid=(D // tn, t // tm),
        in_specs=[
            pl.BlockSpec((tm, D_FF), lambda j, i: (i, 0)),
            pl.BlockSpec((None, D_FF, tn), lambda j, i: (layer, 0, j)),
            pl.BlockSpec((tm, tn), lambda j, i: (i, j)),
            _mod_spec(layer, 5, row_of, tn, lambda j, i: j),
        ],
        out_specs=pl.BlockSpec((tm, tn), lambda j, i: (i, j)),
        scratch_shapes=[pltpu.VMEM((D_FF, tn), BF16)],
        compiler_params=_cp(("arbitrary", "arbitrary"), 54),
        name="ffn_down",
    )(g, w_down, x, mods5)


def kernel(x, c, ctx, c_ctx, w_ada, b_ada, g_mix, w_in, g_q, g_k, w_gmlp, b_gmlp, g_gmlp_v,
           w_sconv, w_pa, w_pb, w_pc, w_o, g_ffn, w_up, w_ffn_conv, w_down, g_final):
    cvec = jnp.concatenate([c, c_ctx[None, :], jnp.zeros((3, D), F32)], axis=0)
    mods5 = _mods(cvec, w_ada, b_ada).reshape(DEPTH, 8, 6, 1, D)
    rope_tabs = _rope_tables()
    q_scale = HD ** -0.5 * math.log2(math.e)

    xs = x.reshape(BATCH * SEQ, D)
    cs = ctx.reshape(BATCH * CTX, D)

    for l in range(DEPTH):
        last = l == DEPTH - 1
        bs = jnp.broadcast_to(b_gmlp[l][:, :, None], (GG, CHUNK, CHUNK))
        hx = _normmod(xs, g_mix[l], mods5, l, False)
        hc = _normmod(cs, g_mix[l], mods5, l, True)

        raw_x = _inproj(hx, w_in, l, COLS_RAW, "raw")
        zg_x = _inproj(hx, w_in, l, COLS_GELU, "gelu")
        gates_x = _inproj(hx, w_in, l, COLS_GATE, "sigmoid")
        if not last:
            raw_c = _inproj(hc, w_in, l, COLS_RAW, "raw")
            kc = _headnorm(raw_c, R_K // KV_DIM, N_KV, g_k[l], 1.0, None)
            vc = raw_c[:, R_V:R_V + KV_DIM]
        else:
            kv_c = _inproj(hc, w_in, l, COLS_KV, "raw")
            kc = _headnorm(kv_c, 0, N_KV, g_k[l], 1.0, None)
            vc = kv_c[:, KV_DIM:]
        qx = _headnorm(raw_x, 0, N_HEADS, g_q[l], q_scale, rope_tabs)
        kx = _headnorm(raw_x, R_K // KV_DIM, N_KV, g_k[l], 1.0, rope_tabs)
        vx = raw_x[:, R_V:R_V + KV_DIM]
        kc3 = kc.reshape(BATCH, CTX, KV_DIM)
        vc3 = vc.reshape(BATCH, CTX, KV_DIM)
        k_all = jnp.concatenate([kx.reshape(BATCH, SEQ, KV_DIM), kc3], axis=1)
        vt_all = jnp.swapaxes(
            jnp.concatenate([vx.reshape(BATCH, SEQ, KV_DIM), vc3], axis=1), 1, 2)
        attn_x = _attention(qx, k_all, vt_all, SEQ)
        ya = _gmlp(zg_x, g_gmlp_v[l], w_gmlp[l], bs)
        yb = _sconv(raw_x, w_sconv[l], SEQ, SEQ)
        m = _merge(ya, yb, attn_x, gates_x, w_pa, w_pb, w_pc, l)
        x1, fx = _outproj(m, w_o, xs, g_ffn[l], mods5, l, False)

        gx = _ffn_up(fx, w_up, w_ffn_conv, l, SEQ, SEQ)
        xs = _ffn_down(gx, w_down, x1, mods5, l, False)

        if not last:
            zg_c = _inproj(hc, w_in, l, COLS_GELU, "gelu")
            gates_c = _inproj(hc, w_in, l, COLS_GATE, "sigmoid")
            qc = _headnorm(raw_c, 0, N_HEADS, g_q[l], q_scale, None)
            attn_c = _attention(qc, kc3, jnp.swapaxes(vc3, 1, 2), CTX)
            yac = _gmlp(zg_c, g_gmlp_v[l], w_gmlp[l], bs)
            ybc = _sconv(raw_c, w_sconv[l], CTX, BATCH * CTX)
            mc = _merge(yac, ybc, attn_c, gates_c, w_pa, w_pb, w_pc, l)
            c1, fc = _outproj(mc, w_o, cs, g_ffn[l], mods5, l, True)
            gc = _ffn_up(fc, w_up, w_ffn_conv, l, CTX, BATCH * CTX)
            cs = _ffn_down(gc, w_down, c1, mods5, l, True)

    return _final_norm(xs, g_final).reshape(BATCH, SEQ, D)
```

```python
import functools
import math

import jax
import jax.numpy as jnp
from jax.experimental import pallas as pl
from jax.experimental.pallas import tpu as pltpu

F32 = jnp.float32
BF16 = jnp.bfloat16

D = 2048
BATCH = 4
SEQ = 2048
CTX = 256
DEPTH = 2
GRID_W = 64
N_HEADS = 16
N_KV = 4
HD = 128
GROUP = N_HEADS // N_KV
CHUNK = 128
GW = D // 2
GG = 8
CW = D // 2
D_FF = 5504
EPS = 1e-6
ROPE_THETA = 10000.0
KV_DIM = N_KV * HD
IN_DIM = D + 2 * KV_DIM + 2 * GW + 3 * CW + 3 * D
OFF_K = D
OFF_U = D + 2 * KV_DIM
OFF_B = OFF_U + 2 * GW
OFF_G = OFF_B + 3 * CW
R_K = D
R_V = D + KV_DIM
R_B = D + 2 * KV_DIM
R_C = R_B + CW
R_H = R_C + CW
LANES = 128

VMEM_PHYS_V7X = 64 * 1024 * 1024
VMEM_CAP = VMEM_PHYS_V7X - 8 * 1024 * 1024


def _cp(sem, vmem_mb):
    return pltpu.CompilerParams(
        dimension_semantics=sem,
        vmem_limit_bytes=min(int(vmem_mb * 1024 * 1024), VMEM_CAP))


def _rms(xf, g):
    ms = jnp.mean(xf * xf, axis=-1, keepdims=True)
    return xf * jax.lax.rsqrt(ms + EPS) * g


def _sigmoid(x):
    return 0.5 * jnp.tanh(0.5 * x) + 0.5


def _gelu_tanh(x):
    c = math.sqrt(2.0 / math.pi)
    return 0.5 * x * (1.0 + jnp.tanh(c * (x + 0.044715 * (x * x * x))))


def _mods_kernel(c_ref, w_ref, b_ref, o_ref):
    c = c_ref[...]
    s = (c * _sigmoid(c)).astype(BF16)
    w = w_ref[...].astype(BF16)
    o_ref[...] = jnp.dot(s, w, preferred_element_type=F32) + b_ref[...]


def _mods(cvec, w_ada, b_ada):
    tn = 1024
    nj = 6 * D // tn
    return pl.pallas_call(
        _mods_kernel,
        out_shape=jax.ShapeDtypeStruct((DEPTH, 8, 6 * D), F32),
        grid=(DEPTH, nj),
        in_specs=[
            pl.BlockSpec((8, D), lambda l, j: (0, 0)),
            pl.BlockSpec((None, D, tn), lambda l, j: (l, 0, j)),
            pl.BlockSpec((None, 1, tn), lambda l, j: (l, 0, j)),
        ],
        out_specs=pl.BlockSpec((None, 8, tn), lambda l, j: (l, 0, j)),
        compiler_params=_cp(("arbitrary", "arbitrary"), 40),
        name="adaln_mods",
    )(cvec, w_ada, b_ada.reshape(DEPTH, 1, 6 * D))


def _mod_spec(layer, k, row_of, width=D, col_of=None):
    if col_of is None:
        return pl.BlockSpec((None, None, None, 1, width),
                            lambda *g: (layer, row_of(*g), k, 0, 0))
    return pl.BlockSpec((None, None, None, 1, width),
                        lambda *g: (layer, row_of(*g), k, 0, col_of(*g)))


def _row_of(ctx, tiles_per_batch, axis):
    if ctx:
        return lambda *g: BATCH
    return lambda *g: g[axis] // tiles_per_batch


def _normmod_kernel(x_ref, g_ref, sh_ref, sc_ref, o_ref):
    y = _rms(x_ref[...], g_ref[...])
    o_ref[...] = (y * (1.0 + sc_ref[...]) + sh_ref[...]).astype(o_ref.dtype)


def _norm_kernel(x_ref, g_ref, o_ref):
    o_ref[...] = _rms(x_ref[...], g_ref[...]).astype(o_ref.dtype)


def _normmod(x, g, mods5, layer, ctx):
    t = x.shape[0]
    tm = 1024
    row_of = _row_of(ctx, SEQ // tm, 0)
    return pl.pallas_call(
        _normmod_kernel,
        out_shape=jax.ShapeDtypeStruct((t, D), BF16),
        grid=(t // tm,),
        in_specs=[
            pl.BlockSpec((tm, D), lambda i: (i, 0)),
            pl.BlockSpec((1, D), lambda i: (0, 0)),
            _mod_spec(layer, 0, row_of),
            _mod_spec(layer, 1, row_of),
        ],
        out_specs=pl.BlockSpec((tm, D), lambda i: (i, 0)),
        compiler_params=_cp(("arbitrary",), 44),
        name="norm_mod",
    )(x, g.reshape(1, D), mods5, mods5)


def _final_norm(x, g):
    t = x.shape[0]
    tm = 1024
    return pl.pallas_call(
        _norm_kernel,
        out_shape=jax.ShapeDtypeStruct((t, D), F32),
        grid=(t // tm,),
        in_specs=[
            pl.BlockSpec((tm, D), lambda i: (i, 0)),
            pl.BlockSpec((1, D), lambda i: (0, 0)),
        ],
        out_specs=pl.BlockSpec((tm, D), lambda i: (i, 0)),
        compiler_params=_cp(("arbitrary",), 48),
        name="final_norm",
    )(x, g.reshape(1, D))


TN_IN = 1024
IN_MCHUNK = 1024
_ACTS ={"raw": lambda a: a, "gelu": _gelu_tanh, "sigmoid": _sigmoid}


def _inproj_kernel(h_ref, w_ref, o_ref, wb_ref, *, act):
    @pl.when(pl.program_id(1) == 0)
    def _():
        wb_ref[...] = w_ref[...].astype(BF16)

    w = wb_ref[...]
    for r0 in range(0, h_ref.shape[0], IN_MCHUNK):
        acc = jnp.dot(h_ref[r0:r0 + IN_MCHUNK, :], w, preferred_element_type=F32)
        o_ref[r0:r0 + IN_MCHUNK, :] = _ACTS[act](acc).astype(o_ref.dtype)


def _inproj(h, w, layer, col_tiles, act):
    t = h.shape[0]
    tm = min(t, 2048)
    first, nj, skip_at, skip = col_tiles

    def wmap(j, i):
        return (layer, 0, first + j + skip * (j // skip_at))

    return pl.pallas_call(
        functools.partial(_inproj_kernel, act=act),
        out_shape=jax.ShapeDtypeStruct((t, nj * TN_IN), BF16),
        grid=(nj, t // tm),
        in_specs=[
            pl.BlockSpec((tm, D), lambda j, i: (i, 0)),
            pl.BlockSpec((None, D, TN_IN), wmap),
        ],
        out_specs=pl.BlockSpec((tm, TN_IN), lambda j, i: (i, j)),
        scratch_shapes=[pltpu.VMEM((D, TN_IN), BF16)],
        compiler_params=_cp(("arbitrary", "arbitrary"), 56),
        name="in_proj_" + act,
    )(h, w)


COLS_RAW = (0, 6, 3, (OFF_B - OFF_U) // TN_IN)
COLS_GELU = (OFF_U // TN_IN, (OFF_B - OFF_U) // TN_IN, 1 << 20, 0)
COLS_GATE = (OFF_G // TN_IN, 3 * D // TN_IN, 1 << 20, 0)
COLS_KV = (OFF_K // TN_IN, 1, 1 << 20, 0)


def _rope_tables():
    rows = SEQ // GRID_W
    row = jnp.repeat(jnp.arange(rows, dtype=F32), GRID_W)
    col = jnp.tile(jnp.arange(GRID_W, dtype=F32), rows)
    n_freq = HD // 4
    inv_freq = ROPE_THETA ** (-jnp.arange(n_freq, dtype=F32) / n_freq)
    ar = row[:, None] * inv_freq
    ac = col[:, None] * inv_freq
    cos = jnp.concatenate([jnp.cos(ar), jnp.cos(ar), jnp.cos(ac), jnp.cos(ac)], axis=-1)
    sin = jnp.concatenate([-jnp.sin(ar), jnp.sin(ar), -jnp.sin(ac), jnp.sin(ac)], axis=-1)
    return cos, sin


def _headnorm_kernel(x_ref, g_ref, *rest, n_heads, scale, rope):
    if rope:
        cos_ref, sin_ref, o_ref = rest
        cos = cos_ref[...]
        sin = sin_ref[...]
        lane = jax.lax.broadcasted_iota(jnp.int32, cos.shape, 1)
        first_half = (lane % (HD // 2)) < (HD // 4)
    else:
        (o_ref,) = rest
    g = g_ref[...] * scale
    for h in range(n_heads):
        x = x_ref[:, h * HD:(h + 1) * HD].astype(F32)
        y = _rms(x, g)
        if rope:
            partner = jnp.where(first_half,
                                pltpu.roll(y, HD - HD // 4, axis=1),
                                pltpu.roll(y, HD // 4, axis=1))
            y = y * cos + partner * sin
        o_ref[:, h * HD:(h + 1) * HD] = y.astype(o_ref.dtype)


def _headnorm(src, col_blk, n_heads, g, scale, rope_tabs):
    t = src.shape[0]
    tm = 1024
    w = n_heads * HD
    rope = rope_tabs is not None
    in_specs = [
        pl.BlockSpec((tm, w), lambda i: (i, col_blk)),
        pl.BlockSpec((1, HD), lambda i: (0, 0)),
    ]
    args = [src, g.reshape(1, HD)]
    if rope:
        nt = SEQ // tm
        in_specs += [pl.BlockSpec((tm, HD), lambda i: (i % nt, 0))] * 2
        args += list(rope_tabs)
    return pl.pallas_call(
        functools.partial(_headnorm_kernel, n_heads=n_heads, scale=scale, rope=rope),
        out_shape=jax.ShapeDtypeStruct((t, w), BF16),
        grid=(t // tm,),
        in_specs=in_specs,
        out_specs=pl.BlockSpec((tm, w), lambda i: (i, 0)),
        compiler_params=_cp(("arbitrary",), 44),
        name="head_norm",
    )(*args)


TQ_SUB = 256


def _attn_kernel(q_ref, k_ref, vt_ref, o_ref):
    k = k_ref[...]
    vt = vt_ref[...]
    units = [(r0, g0) for r0 in range(0, q_ref.shape[0], TQ_SUB) for g0 in range(0, GROUP, 2)]

    def scores(u):
        r0, g0 = units[u]
        q = jnp.concatenate([q_ref[r0:r0 + TQ_SUB, g0 * HD:(g0 + 1) * HD],
                             q_ref[r0:r0 + TQ_SUB, (g0 + 1) * HD:(g0 + 2) * HD]], axis=0)
        return jax.lax.dot_general(k, q, (((1,), (1,)), ((), ())),
                                   preferred_element_type=F32)

    for u, (r0, g0) in enumerate(units):
        st = scores(u)
        m = jnp.max(st, axis=0, keepdims=True)
        p = jnp.exp2(st - m)
        l = jnp.sum(p, axis=0, keepdims=True)
        ot = jnp.dot(vt, p.astype(BF16), preferred_element_type=F32)
        ot = ot * (1.0 / l)
        o_ref[r0:r0 + TQ_SUB, g0 * HD:(g0 + 1) * HD] = ot[:, :TQ_SUB].T.astype(o_ref.dtype)
        o_ref[r0:r0 + TQ_SUB, (g0 + 1) * HD:(g0 + 2) * HD] = ot[:, TQ_SUB:].T.astype(o_ref.dtype)


def _attention(q, k, vt, lq):
    s_len = k.shape[1]
    tq = min(lq, 1024)
    nq = lq // tq
    return pl.pallas_call(
        _attn_kernel,
        out_shape=jax.ShapeDtypeStruct((BATCH * lq, D), BF16),
        grid=(BATCH, N_KV, nq),
        in_specs=[
            pl.BlockSpec((tq, GROUP * HD), lambda b, h, i: (b * nq + i, h)),
            pl.BlockSpec((None, s_len, HD), lambda b, h, i: (b, 0, h)),
            pl.BlockSpec((None, HD, s_len), lambda b, h, i: (b, h, 0)),
        ],
        out_specs=pl.BlockSpec((tq, GROUP * HD), lambda b, h, i: (b * nq + i, h)),
        compiler_params=_cp(("arbitrary", "arbitrary", "arbitrary"), 48),
        name="attention",
    )(q, k, vt)


def _gmlp_kernel(u_ref, v_ref, gv_ref, ws_ref, bs_ref, o_ref, *, tm):
    v = _rms(v_ref[...].astype(F32), gv_ref[...]).astype(BF16)
    for g in range(GG):
        c0 = g * CHUNK
        w = ws_ref[g].astype(BF16)
        for n in range(tm // CHUNK):
            r0 = n * CHUNK
            mixed = jnp.dot(w, v[r0:r0 + CHUNK, c0:c0 + CHUNK],
                            preferred_element_type=F32) + bs_ref[g]
            u = u_ref[r0:r0 + CHUNK, c0:c0 + CHUNK].astype(F32)
            o_ref[r0:r0 + CHUNK, c0:c0 + CHUNK] = (u * mixed).astype(o_ref.dtype)


def _gmlp(zg, gv, ws, bs):
    t = zg.shape[0]
    tm = 512
    return pl.pallas_call(
        functools.partial(_gmlp_kernel, tm=tm),
        out_shape=jax.ShapeDtypeStruct((t, GW), BF16),
        grid=(t // tm,),
        in_specs=[
            pl.BlockSpec((tm, GW), lambda i: (i, 0)),
            pl.BlockSpec((tm, GW), lambda i: (i, 1)),
            pl.BlockSpec((1, GW), lambda i: (0, 0)),
            pl.BlockSpec((GG, CHUNK, CHUNK), lambda i: (0, 0, 0)),
            pl.BlockSpec((GG, CHUNK, CHUNK), lambda i: (0, 0, 0)),
        ],
        out_specs=pl.BlockSpec((tm, GW), lambda i: (i, 0)),
        compiler_params=_cp(("arbitrary",), 32),
        name="gmlp",
    )(zg, zg, gv.reshape(1, GW), ws, bs)


def _dwconv3(x, w, seq_len):
    rows = x.shape[0]
    pos = jax.lax.broadcasted_iota(jnp.int32, x.shape, 0) % seq_len
    prev = jnp.where(pos == 0, 0.0, pltpu.roll(x, 1, axis=0))
    nxt = jnp.where(pos == seq_len - 1, 0.0, pltpu.roll(x, rows - 1, axis=0))
    return prev * w[0:1, :] + x * w[1:2, :] + nxt * w[2:3, :]


def _sconv_kernel(b_ref, c_ref, h_ref, w_ref, o_ref, *, seq_len):
    t = c_ref[...].astype(F32) * h_ref[...].astype(F32)
    y = b_ref[...].astype(F32) * _dwconv3(t, w_ref[...], seq_len)
    o_ref[...] = y.astype(o_ref.dtype)


def _sconv(raw, w, seq_len, rows):
    t = raw.shape[0]
    tc = 256
    return pl.pallas_call(
        functools.partial(_sconv_kernel, seq_len=seq_len),
        out_shape=jax.ShapeDtypeStruct((t, CW), BF16),
        grid=(t // rows, CW // tc),
        in_specs=[
            pl.BlockSpec((rows, tc), lambda i, j: (i, R_B // tc + j)),
            pl.BlockSpec((rows, tc), lambda i, j: (i, R_C // tc + j)),
            pl.BlockSpec((rows, tc), lambda i, j: (i, R_H // tc + j)),
            pl.BlockSpec((3, tc), lambda i, j: (0, j)),
        ],
        out_specs=pl.BlockSpec((rows, tc), lambda i, j: (i, j)),
        compiler_params=_cp(("arbitrary", "arbitrary"), 40),
        name="short_conv",
    )(raw, raw, raw, w)


def _merge_kernel(ya_ref, yb_ref, at_ref, wa_ref, wb_ref, wc_ref,
                  ga_ref, gb_ref, gc_ref, o_ref, wa_s, wb_s, wc_s):
    @pl.when(pl.program_id(1) == 0)
    def _():
        wa_s[...] = wa_ref[...].astype(BF16)
        wb_s[...] = wb_ref[...].astype(BF16)
        wc_s[...] = wc_ref[...].astype(BF16)

    a = jnp.dot(ya_ref[...], wa_s[...], preferred_element_type=F32)
    m = ga_ref[...].astype(F32) * a
    b = jnp.dot(yb_ref[...], wb_s[...], preferred_element_type=F32)
    m = m + gb_ref[...].astype(F32) * b
    c = jnp.dot(at_ref[...], wc_s[...], preferred_element_type=F32)
    m = m + gc_ref[...].astype(F32) * c
    o_ref[...] = m.astype(o_ref.dtype)


def _merge(ya, yb, attn, gates, w_pa, w_pb, w_pc, layer):
    t = ya.shape[0]
    tm = 1024
    tn = 512
    nb = D // tn
    return pl.pallas_call(
        _merge_kernel,
        out_shape=jax.ShapeDtypeStruct((t, D), BF16),
        grid=(nb, t // tm),
        in_specs=[
            pl.BlockSpec((tm, GW), lambda j, i: (i, 0)),
            pl.BlockSpec((tm, CW), lambda j, i: (i, 0)),
            pl.BlockSpec((tm, D), lambda j, i: (i, 0)),
            pl.BlockSpec((None, GW, tn), lambda j, i: (layer, 0, j)),
            pl.BlockSpec((None, CW, tn), lambda j, i: (layer, 0, j)),
            pl.BlockSpec((None, D, tn), lambda j, i: (layer, 0, j)),
            pl.BlockSpec((tm, tn), lambda j, i: (i, j)),
            pl.BlockSpec((tm, tn), lambda j, i: (i, nb + j)),
            pl.BlockSpec((tm, tn), lambda j, i: (i, 2 * nb + j)),
        ],
        out_specs=pl.BlockSpec((tm, tn), lambda j, i: (i, j)),
        scratch_shapes=[pltpu.VMEM((GW, tn), BF16), pltpu.VMEM((CW, tn), BF16),
                        pltpu.VMEM((D, tn), BF16)],
        compiler_params=_cp(("arbitrary", "arbitrary"), 56),
        name="merge",
    )(ya, yb, attn, w_pa, w_pb, w_pc, gates, gates, gates)


def _outproj_kernel(m_ref, w_ref, x_ref, gt_ref, g_ref, sh_ref, sc_ref, x1_ref, h_ref, w_s):
    @pl.when(pl.program_id(0) == 0)
    def _():
        w_s[...] = w_ref[...].astype(BF16)

    y = jnp.dot(m_ref[...], w_s[...], preferred_element_type=F32)
    x1 = x_ref[...] + gt_ref[...] * y
    x1_ref[...] = x1
    h_ref[...] = (_rms(x1, g_ref[...]) * (1.0 + sc_ref[...]) + sh_ref[...]).astype(h_ref.dtype)


def _outproj(m, w_o, x, g_ffn, mods5, layer, ctx):
    t = m.shape[0]
    tm = 512
    row_of = _row_of(ctx, SEQ // tm, 0)
    return pl.pallas_call(
        _outproj_kernel,
        out_shape=(jax.ShapeDtypeStruct((t, D), F32), jax.ShapeDtypeStruct((t, D), BF16)),
        grid=(t // tm,),
        in_specs=[
            pl.BlockSpec((tm, D), lambda i: (i, 0)),
            pl.BlockSpec((None, D, D), lambda i: (layer, 0, 0), pipeline_mode=pl.Buffered(1)),
            pl.BlockSpec((tm, D), lambda i: (i, 0)),
            _mod_spec(layer, 2, row_of),
            pl.BlockSpec((1, D), lambda i: (0, 0)),
            _mod_spec(layer, 3, row_of),
            _mod_spec(layer, 4, row_of),
        ],
        out_specs=(pl.BlockSpec((tm, D), lambda i: (i, 0)),
                   pl.BlockSpec((tm, D), lambda i: (i, 0))),
        scratch_shapes=[pltpu.VMEM((D, D), BF16)],
        compiler_params=_cp(("arbitrary",), 56),
        name="out_proj",
    )(m, w_o, x, mods5, g_ffn.reshape(1, D), mods5, mods5)


FF_BLOCKS = D_FF // LANES
FF_SUBW = 2 * LANES
FF_NSUB = 2
TN_FF = FF_NSUB * FF_SUBW
FF_NBLK = TN_FF // LANES
FF_MCHUNK = 256


def _chunked_dot(h_ref, w):
    parts = [jnp.dot(h_ref[r0:r0 + FF_MCHUNK, :], w, preferred_element_type=F32)
             for r0 in range(0, h_ref.shape[0], FF_MCHUNK)]
    return jnp.concatenate(parts, axis=0)


def _ffn_up_kernel(h_ref, *refs, seq_len):
    a_refs = refs[0:FF_NBLK]
    b_refs = refs[FF_NBLK:2 * FF_NBLK]
    ca_refs = refs[2 * FF_NBLK:3 * FF_NBLK]
    cb_refs = refs[3 * FF_NBLK:4 * FF_NBLK]
    o_ref = refs[4 * FF_NBLK]
    ws = []
    for s in range(FF_NSUB):
        k0 = s * (FF_SUBW // LANES)
        k1 = k0 + FF_SUBW // LANES
        ws.append((jnp.concatenate([r[...] for r in a_refs[k0:k1]], axis=1).astype(BF16),
                   jnp.concatenate([r[...] for r in b_refs[k0:k1]], axis=1).astype(BF16),
                   jnp.concatenate([r[...] for r in ca_refs[k0:k1]], axis=1),
                   jnp.concatenate([r[...] for r in cb_refs[k0:k1]], axis=1)))
    for s in range(FF_NSUB):
        wa, wb, ca, cb = ws[s]
        a = _dwconv3(_chunked_dot(h_ref, wa), ca, seq_len)
        b = _dwconv3(_chunked_dot(h_ref, wb), cb, seq_len)
        o_ref[:, s * FF_SUBW:(s + 1) * FF_SUBW] = (a * _sigmoid(a) * b).astype(o_ref.dtype)


def _ffn_up(h, w_up, w_conv, layer, seq_len, rows):
    t = h.shape[0]
    nj = pl.cdiv(D_FF, TN_FF)
    last = 2 * FF_BLOCKS - 1

    def wspec(rows_, off, k):
        return pl.BlockSpec((None, rows_, LANES),
                            lambda i, j: (layer, 0, jnp.minimum(off + FF_NBLK * j + k, last)))

    blocks = range(FF_NBLK)
    specs = ([wspec(D, 0, k) for k in blocks] + [wspec(D, FF_BLOCKS, k) for k in blocks]
             + [wspec(3, 0, k) for k in blocks] + [wspec(3, FF_BLOCKS, k) for k in blocks])
    return pl.pallas_call(
        functools.partial(_ffn_up_kernel, seq_len=seq_len),
        out_shape=jax.ShapeDtypeStruct((t, D_FF), BF16),
        grid=(t // rows, nj),
        in_specs=[pl.BlockSpec((rows, D), lambda i, j: (i, 0))] + specs,
        out_specs=pl.BlockSpec((rows, TN_FF), lambda i, j: (i, j)),
        compiler_params=_cp(("arbitrary", "arbitrary"), 56),
        name="ffn_up",
    )(h, *([w_up] * (2 * FF_NBLK)), *([w_conv] * (2 * FF_NBLK)))


def _ffn_down_kernel(g_ref, w_ref, x_ref, gt_ref, o_ref, w_s):
    @pl.when(pl.program_id(1) == 0)
    def _():
        w_s[...] = w_ref[...].astype(BF16)

    y = jnp.dot(g_ref[...], w_s[...], preferred_element_type=F32)
    o_ref[...] = x_ref[...] + gt_ref[...] * y


def _ffn_down(g, w_down, x, mods5, layer, ctx):
    t = g.shape[0]
    tm = 512
    tn = 512
    row_of = _row_of(ctx, SEQ // tm, 1)
    return pl.pallas_call(
        _ffn_down_kernel,
        out_shape=jax.ShapeDtypeStruct((t, D), F32),
        grid=(D // tn, t // tm),
        in_specs=[
            pl.BlockSpec((tm, D_FF), lambda j, i: (i, 0)),
            pl.BlockSpec((None, D_FF, tn), lambda j, i: (layer, 0, j)),
            pl.BlockSpec((tm, tn), lambda j, i: (i, j)),
            _mod_spec(layer, 5, row_of, tn, lambda j, i: j),
        ],
        out_specs=pl.BlockSpec((tm, tn), lambda j, i: (i, j)),
        scratch_shapes=[pltpu.VMEM((D_FF, tn), BF16)],
        compiler_params=_cp(("arbitrary", "arbitrary"), 54),
        name="ffn_down",
    )(g, w_down, x, mods5)


def kernel(x, c, ctx, c_ctx, w_ada, b_ada, g_mix, w_in, g_q, g_k, w_gmlp, b_gmlp, g_gmlp_v,
           w_sconv, w_pa, w_pb, w_pc, w_o, g_ffn, w_up, w_ffn_conv, w_down, g_final):
    cvec = jnp.concatenate([c, c_ctx[None, :], jnp.zeros((3, D), F32)], axis=0)
    mods5 = _mods(cvec, w_ada, b_ada).reshape(DEPTH, 8, 6, 1, D)
    rope_tabs = _rope_tables()
    q_scale = HD ** -0.5 * math.log2(math.e)

    xs = x.reshape(BATCH * SEQ, D)
    cs = ctx.reshape(BATCH * CTX, D)

    for l in range(DEPTH):
        last = l == DEPTH - 1
        bs = jnp.broadcast_to(b_gmlp[l][:, :, None], (GG, CHUNK, CHUNK))
        hx = _normmod(xs, g_mix[l], mods5, l, False)
        hc = _normmod(cs, g_mix[l], mods5, l, True)

        raw_x = _inproj(hx, w_in, l, COLS_RAW, "raw")
        zg_x = _inproj(hx, w_in, l, COLS_GELU, "gelu")
        gates_x = _inproj(hx, w_in, l, COLS_GATE, "sigmoid")
        if not last:
            raw_c = _inproj(hc, w_in, l, COLS_RAW, "raw")
            kc = _headnorm(raw_c, R_K // KV_DIM, N_KV, g_k[l], 1.0, None)
            vc = raw_c[:, R_V:R_V + KV_DIM]
        else:
            kv_c = _inproj(hc, w_in, l, COLS_KV, "raw")
            kc = _headnorm(kv_c, 0, N_KV, g_k[l], 1.0, None)
            vc = kv_c[:, KV_DIM:]
        qx = _headnorm(raw_x, 0, N_HEADS, g_q[l], q_scale, rope_tabs)
        kx = _headnorm(raw_x, R_K // KV_DIM, N_KV, g_k[l], 1.0, rope_tabs)
        vx = raw_x[:, R_V:R_V + KV_DIM]
        kc3 = kc.reshape(BATCH, CTX, KV_DIM)
        vc3 = vc.reshape(BATCH, CTX, KV_DIM)
        k_all = jnp.concatenate([kx.reshape(BATCH, SEQ, KV_DIM), kc3], axis=1)
        vt_all = jnp.swapaxes(
            jnp.concatenate([vx.reshape(BATCH, SEQ, KV_DIM), vc3], axis=1), 1, 2)
        attn_x = _attention(qx, k_all, vt_all, SEQ)
        ya = _gmlp(zg_x, g_gmlp_v[l], w_gmlp[l], bs)
        yb = _sconv(raw_x, w_sconv[l], SEQ, SEQ)
        m = _merge(ya, yb, attn_x, gates_x, w_pa, w_pb, w_pc, l)
        x1, fx = _outproj(m, w_o, xs, g_ffn[l], mods5, l, False)

        gx = _ffn_up(fx, w_up, w_ffn_conv, l, SEQ, SEQ)
        xs = _ffn_down(gx, w_down, x1, mods5, l, False)

        if not last:
            zg_c = _inproj(hc, w_in, l, COLS_GELU, "gelu")
            gates_c = _inproj(hc, w_in, l, COLS_GATE, "sigmoid")
            qc = _headnorm(raw_c, 0, N_HEADS, g_q[l], q_scale, None)
            attn_c = _attention(qc, kc3, jnp.swapaxes(vc3, 1, 2), CTX)
            yac = _gmlp(zg_c, g_gmlp_v[l], w_gmlp[l], bs)
            ybc = _sconv(raw_c, w_sconv[l], CTX, BATCH * CTX)
            mc = _merge(yac, ybc, attn_c, gates_c, w_pa, w_pb, w_pc, l)
            c1, fc = _outproj(mc, w_o, cs, g_ffn[l], mods5, l, True)
            gc = _ffn_up(fc, w_up, w_ffn_conv, l, CTX, BATCH * CTX)
            cs = _ffn_down(gc, w_down, c1, mods5, l, True)

    return _final_norm(xs, g_final).reshape(BATCH, SEQ, D)
```

```python
import functools
import math

import jax
import jax.numpy as jnp
from jax.experimental import pallas as pl
from jax.experimental.pallas import tpu as pltpu

F32 = jnp.float32
BF16 = jnp.bfloat16

D = 2048
BATCH = 4
SEQ = 2048
CTX = 256
DEPTH = 2
GRID_W = 64
N_HEADS = 16
N_KV = 4
HD = 128
GROUP = N_HEADS // N_KV
CHUNK = 128
GW = D // 2
GG = 8
CW = D // 2
D_FF = 5504
EPS = 1e-6
ROPE_THETA = 10000.0
KV_DIM = N_KV * HD
IN_DIM = D + 2 * KV_DIM + 2 * GW + 3 * CW + 3 * D
OFF_K = D
OFF_U = D + 2 * KV_DIM
OFF_B = OFF_U + 2 * GW
OFF_G = OFF_B + 3 * CW
R_K = D
R_V = D + KV_DIM
R_B = D + 2 * KV_DIM
R_C = R_B + CW
R_H = R_C + CW
LANES = 128

VMEM_PHYS_V7X = 64 * 1024 * 1024
VMEM_CAP = VMEM_PHYS_V7X - 8 * 1024 * 1024


def _cp(sem, vmem_mb):
    return pltpu.CompilerParams(
        dimension_semantics=sem,
        vmem_limit_bytes=min(int(vmem_mb * 1024 * 1024), VMEM_CAP))


def _rms(xf, g):
    ms = jnp.mean(xf * xf, axis=-1, keepdims=True)
    return xf * jax.lax.rsqrt(ms + EPS) * g


def _sigmoid(x):
    return 0.5 * jnp.tanh(0.5 * x) + 0.5


def _gelu_tanh(x):
    c = math.sqrt(2.0 / math.pi)
    return 0.5 * x * (1.0 + jnp.tanh(c * (x + 0.044715 * (x * x * x))))


def _mods_kernel(c_ref, w_ref, b_ref, o_ref):
    c = c_ref[...]
    s = (c * _sigmoid(c)).astype(BF16)
    w = w_ref[...].astype(BF16)
    o_ref[...] = jnp.dot(s, w, preferred_element_type=F32) + b_ref[...]


def _mods(cvec, w_ada, b_ada):
    tn = 1024
    nj = 6 * D // tn
    return pl.pallas_call(
        _mods_kernel,
        out_shape=jax.ShapeDtypeStruct((DEPTH, 8, 6 * D), F32),
        grid=(DEPTH, nj),
        in_specs=[
            pl.BlockSpec((8, D), lambda l, j: (0, 0)),
            pl.BlockSpec((None, D, tn), lambda l, j: (l, 0, j)),
            pl.BlockSpec((None, 1, tn), lambda l, j: (l, 0, j)),
        ],
        out_specs=pl.BlockSpec((None, 8, tn), lambda l, j: (l, 0, j)),
        compiler_params=_cp(("arbitrary", "arbitrary"), 40),
        name="adaln_mods",
    )(cvec, w_ada, b_ada.reshape(DEPTH, 1, 6 * D))


def _mod_spec(layer, k, row_of, width=D, col_of=None):
    if col_of is None:
        return pl.BlockSpec((None, None, None, 1, width),
                            lambda *g: (layer, row_of(*g), k, 0, 0))
    return pl.BlockSpec((None, None, None, 1, width),
                        lambda *g: (layer, row_of(*g), k, 0, col_of(*g)))


def _row_of(ctx, tiles_per_batch, axis):
    if ctx:
        return lambda *g: BATCH
    return lambda *g: g[axis] // tiles_per_batch


def _normmod_kernel(x_ref, g_ref, sh_ref, sc_ref, o_ref):
    y = _rms(x_ref[...], g_ref[...])
    o_ref[...] = (y * (1.0 + sc_ref[...]) + sh_ref[...]).astype(o_ref.dtype)


def _norm_kernel(x_ref, g_ref, o_ref):
    o_ref[...] = _rms(x_ref[...], g_ref[...]).astype(o_ref.dtype)


def _normmod(x, g, mods5, layer, ctx):
    t = x.shape[0]
    tm = 1024
    row_of = _row_of(ctx, SEQ // tm, 0)
    return pl.pallas_call(
        _normmod_kernel,
        out_shape=jax.ShapeDtypeStruct((t, D), BF16),
        grid=(t // tm,),
        in_specs=[
            pl.BlockSpec((tm, D), lambda i: (i, 0)),
            pl.BlockSpec((1, D), lambda i: (0, 0)),
            _mod_spec(layer, 0, row_of),
            _mod_spec(layer, 1, row_of),
        ],
        out_specs=pl.BlockSpec((tm, D), lambda i: (i, 0)),
        compiler_params=_cp(("arbitrary",), 44),
        name="norm_mod",
    )(x, g.reshape(1, D), mods5, mods5)


def _final_norm(x, g):
    t = x.shape[0]
    tm = 1024
    return pl.pallas_call(
        _norm_kernel,
        out_shape=jax.ShapeDtypeStruct((t, D), F32),
        grid=(t // tm,),
        in_specs=[
            pl.BlockSpec((tm, D), lambda i: (i, 0)),
            pl.BlockSpec((1, D), lambda i: (0, 0)),
        ],
        out_specs=pl.BlockSpec((tm, D), lambda i: (i, 0)),
        compiler_params=_cp(("arbitrary",), 48),
        name="final_norm",
    )(x, g.reshape(1, D))


TN_IN = 1024
IN_MCHUNK = 1024
_ACTS ={"raw": lambda a: a, "gelu": _gelu_tanh, "sigmoid": _sigmoid}


def _inproj_kernel(h_ref, w_ref, o_ref, wb_ref, *, act):
    @pl.when(pl.program_id(1) == 0)
    def _():
        wb_ref[...] = w_ref[...].astype(BF16)

    w = wb_ref[...]
    for r0 in range(0, h_ref.shape[0], IN_MCHUNK):
        acc = jnp.dot(h_ref[r0:r0 + IN_MCHUNK, :], w, preferred_element_type=F32)
        o_ref[r0:r0 + IN_MCHUNK, :] = _ACTS[act](acc).astype(o_ref.dtype)


def _inproj(h, w, layer, col_tiles, act):
    t = h.shape[0]
    tm = min(t, 2048)
    first, nj, skip_at, skip = col_tiles

    def wmap(j, i):
        return (layer, 0, first + j + skip * (j // skip_at))

    return pl.pallas_call(
        functools.partial(_inproj_kernel, act=act),
        out_shape=jax.ShapeDtypeStruct((t, nj * TN_IN), BF16),
        grid=(nj, t // tm),
        in_specs=[
            pl.BlockSpec((tm, D), lambda j, i: (i, 0)),
            pl.BlockSpec((None, D, TN_IN), wmap),
        ],
        out_specs=pl.BlockSpec((tm, TN_IN), lambda j, i: (i, j)),
        scratch_shapes=[pltpu.VMEM((D, TN_IN), BF16)],
        compiler_params=_cp(("arbitrary", "arbitrary"), 56),
        name="in_proj_" + act,
    )(h, w)


COLS_RAW = (0, 6, 3, (OFF_B - OFF_U) // TN_IN)
COLS_GELU = (OFF_U // TN_IN, (OFF_B - OFF_U) // TN_IN, 1 << 20, 0)
COLS_GATE = (OFF_G // TN_IN, 3 * D // TN_IN, 1 << 20, 0)
COLS_KV = (OFF_K // TN_IN, 1, 1 << 20, 0)


def _rope_tables():
    rows = SEQ // GRID_W
    row = jnp.repeat(jnp.arange(rows, dtype=F32), GRID_W)
    col = jnp.tile(jnp.arange(GRID_W, dtype=F32), rows)
    n_freq = HD // 4
    inv_freq = ROPE_THETA ** (-jnp.arange(n_freq, dtype=F32) / n_freq)
    ar = row[:, None] * inv_freq
    ac = col[:, None] * inv_freq
    cos = jnp.concatenate([jnp.cos(ar), jnp.cos(ar), jnp.cos(ac), jnp.cos(ac)], axis=-1)
    sin = jnp.concatenate([-jnp.sin(ar), jnp.sin(ar), -jnp.sin(ac), jnp.sin(ac)], axis=-1)
    return cos, sin


def _headnorm_kernel(x_ref, g_ref, *rest, n_heads, scale, rope):
    if rope:
        cos_ref, sin_ref, o_ref = rest
        cos = cos_ref[...]
        sin = sin_ref[...]
        lane = jax.lax.broadcasted_iota(jnp.int32, cos.shape, 1)
        first_half = (lane % (HD // 2)) < (HD // 4)
    else:
        (o_ref,) = rest
    g = g_ref[...] * scale
    for h in range(n_heads):
        x = x_ref[:, h * HD:(h + 1) * HD].astype(F32)
        y = _rms(x, g)
        if rope:
            partner = jnp.where(first_half,
                                pltpu.roll(y, HD - HD // 4, axis=1),
                                pltpu.roll(y, HD // 4, axis=1))
            y = y * cos + partner * sin
        o_ref[:, h * HD:(h + 1) * HD] = y.astype(o_ref.dtype)


def _headnorm(src, col_blk, n_heads, g, scale, rope_tabs):
    t = src.shape[0]
    tm = 1024
    w = n_heads * HD
    rope = rope_tabs is not None
    in_specs = [
        pl.BlockSpec((tm, w), lambda i: (i, col_blk)),
        pl.BlockSpec((1, HD), lambda i: (0, 0)),
    ]
    args = [src, g.reshape(1, HD)]
    if rope:
        nt = SEQ // tm
        in_specs += [pl.BlockSpec((tm, HD), lambda i: (i % nt, 0))] * 2
        args += list(rope_tabs)
    return pl.pallas_call(
        functools.partial(_headnorm_kernel, n_heads=n_heads, scale=scale, rope=rope),
        out_shape=jax.ShapeDtypeStruct((t, w), BF16),
        grid=(t // tm,),
        in_specs=in_specs,
        out_specs=pl.BlockSpec((tm, w), lambda i: (i, 0)),
        compiler_params=_cp(("arbitrary",), 44),
        name="head_norm",
    )(*args)


TQ_SUB = 256


KEY_TILE = 256


def _attn_kernel(q_ref, *refs, with_latent):
    if with_latent:
        kx_ref, vx_ref, kc_ref, vc_ref, o_ref, k_s, vt_s = refs
        k_s[0:SEQ, :] = kx_ref[...]
        k_s[SEQ:SEQ + CTX, :] = kc_ref[...]
        vt_s[:, 0:SEQ] = vx_ref[...].T
        vt_s[:, SEQ:SEQ + CTX] = vc_ref[...].T
    else:
        kc_ref, vc_ref, o_ref, k_s, vt_s = refs
        k_s[...] = kc_ref[...]
        vt_s[...] = vc_ref[...].T
    tiles = list(range(0, k_s.shape[0], KEY_TILE))
    units = [(r0, g0) for r0 in range(0, q_ref.shape[0], TQ_SUB) for g0 in range(0, GROUP, 2)]

    def queries(u):
        r0, g0 = units[u]
        return jnp.concatenate([q_ref[r0:r0 + TQ_SUB, g0 * HD:(g0 + 1) * HD],
                                q_ref[r0:r0 + TQ_SUB, (g0 + 1) * HD:(g0 + 2) * HD]], axis=0)

    def scores(q, t0):
        return jax.lax.dot_general(k_s[t0:t0 + KEY_TILE, :], q, (((1,), (1,)), ((), ())),
                                   preferred_element_type=F32)

    def colmax(sts):
        m = None
        for st in sts:
            mc = jnp.max(st, axis=0, keepdims=True)
            m = mc if m is None else jnp.maximum(m, mc)
        return m

    sts = [scores(queries(0), t0) for t0 in tiles]
    m = colmax(sts)
    for u, (r0, g0) in enumerate(units):
        q_next = queries(u + 1) if u + 1 < len(units) else None
        sts_next = []
        l = None
        ot = None
        for t0, st in zip(tiles, sts):
            p = jnp.exp2(st - m)
            lc = jnp.sum(p, axis=0, keepdims=True)
            oc = jnp.dot(vt_s[:, t0:t0 + KEY_TILE], p.astype(BF16),
                         preferred_element_type=F32)
            l = lc if l is None else l + lc
            ot = oc if ot is None else ot + oc
            if q_next is not None:
                sts_next.append(scores(q_next, t0))
        ot = ot * (1.0 / l)
        o_ref[r0:r0 + TQ_SUB, g0 * HD:(g0 + 1) * HD] = ot[:, :TQ_SUB].T.astype(o_ref.dtype)
        o_ref[r0:r0 + TQ_SUB, (g0 + 1) * HD:(g0 + 2) * HD] = ot[:, TQ_SUB:].T.astype(o_ref.dtype)
        if q_next is not None:
            sts = sts_next
            m = colmax(sts)


def _attention(q, lq, latent, context):
    with_latent = latent is not None
    s_len = (SEQ if with_latent else 0) + CTX
    tq = min(lq, 1024)
    nq = lq // tq
    in_specs = [pl.BlockSpec((tq, GROUP * HD), lambda b, h, i: (b * nq + i, h))]
    args = [q]
    for src, rows in ((latent, SEQ), (context, CTX)):
        if src is None:
            continue
        k_arr, v_arr, v_blk = src
        in_specs += [pl.BlockSpec((rows, HD), lambda b, h, i: (b, h)),
                     pl.BlockSpec((rows, HD), lambda b, h, i, v_blk=v_blk: (b, v_blk + h))]
        args += [k_arr, v_arr]
    return pl.pallas_call(
        functools.partial(_attn_kernel, with_latent=with_latent),
        out_shape=jax.ShapeDtypeStruct((BATCH * lq, D), BF16),
        grid=(BATCH, N_KV, nq),
        in_specs=in_specs,
        out_specs=pl.BlockSpec((tq, GROUP * HD), lambda b, h, i: (b * nq + i, h)),
        scratch_shapes=[pltpu.VMEM((s_len, HD), BF16), pltpu.VMEM((HD, s_len), BF16)],
        compiler_params=_cp(("arbitrary", "arbitrary", "arbitrary"), 48),
        name="attention",
    )(*args)


def _gmlp_kernel(u_ref, v_ref, gv_ref, ws_ref, bs_ref, o_ref, *, tm):
    v = _rms(v_ref[...].astype(F32), gv_ref[...]).astype(BF16)
    for g in range(GG):
        c0 = g * CHUNK
        w = ws_ref[g].astype(BF16)
        for n in range(tm // CHUNK):
            r0 = n * CHUNK
            mixed = jnp.dot(w, v[r0:r0 + CHUNK, c0:c0 + CHUNK],
                            preferred_element_type=F32) + bs_ref[g]
            u = u_ref[r0:r0 + CHUNK, c0:c0 + CHUNK].astype(F32)
            o_ref[r0:r0 + CHUNK, c0:c0 + CHUNK] = (u * mixed).astype(o_ref.dtype)


def _gmlp(zg, gv, ws, bs):
    t = zg.shape[0]
    tm = 512
    return pl.pallas_call(
        functools.partial(_gmlp_kernel, tm=tm),
        out_shape=jax.ShapeDtypeStruct((t, GW), BF16),
        grid=(t // tm,),
        in_specs=[
            pl.BlockSpec((tm, GW), lambda i: (i, 0)),
            pl.BlockSpec((tm, GW), lambda i: (i, 1)),
            pl.BlockSpec((1, GW), lambda i: (0, 0)),
            pl.BlockSpec((GG, CHUNK, CHUNK), lambda i: (0, 0, 0)),
            pl.BlockSpec((GG, CHUNK, CHUNK), lambda i: (0, 0, 0)),
        ],
        out_specs=pl.BlockSpec((tm, GW), lambda i: (i, 0)),
        compiler_params=_cp(("arbitrary",), 32),
        name="gmlp",
    )(zg, zg, gv.reshape(1, GW), ws, bs)


def _dwconv3(x, w, seq_len):
    rows = x.shape[0]
    pos = jax.lax.broadcasted_iota(jnp.int32, x.shape, 0) % seq_len
    prev = jnp.where(pos == 0, 0.0, pltpu.roll(x, 1, axis=0))
    nxt = jnp.where(pos == seq_len - 1, 0.0, pltpu.roll(x, rows - 1, axis=0))
    return prev * w[0:1, :] + x * w[1:2, :] + nxt * w[2:3, :]


def _sconv_kernel(b_ref, c_ref, h_ref, w_ref, o_ref, *, seq_len):
    t = c_ref[...].astype(F32) * h_ref[...].astype(F32)
    y = b_ref[...].astype(F32) * _dwconv3(t, w_ref[...], seq_len)
    o_ref[...] = y.astype(o_ref.dtype)


def _sconv(raw, w, seq_len, rows):
    t = raw.shape[0]
    tc = 256
    return pl.pallas_call(
        functools.partial(_sconv_kernel, seq_len=seq_len),
        out_shape=jax.ShapeDtypeStruct((t, CW), BF16),
        grid=(t // rows, CW // tc),
        in_specs=[
            pl.BlockSpec((rows, tc), lambda i, j: (i, R_B // tc + j)),
            pl.BlockSpec((rows, tc), lambda i, j: (i, R_C // tc + j)),
            pl.BlockSpec((rows, tc), lambda i, j: (i, R_H // tc + j)),
            pl.BlockSpec((3, tc), lambda i, j: (0, j)),
        ],
        out_specs=pl.BlockSpec((rows, tc), lambda i, j: (i, j)),
        compiler_params=_cp(("arbitrary", "arbitrary"), 40),
        name="short_conv",
    )(raw, raw, raw, w)


def _merge_kernel(ya_ref, yb_ref, at_ref, wa_ref, wb_ref, wc_ref,
                  ga_ref, gb_ref, gc_ref, o_ref, wa_s, wb_s, wc_s):
    @pl.when(pl.program_id(1) == 0)
    def _():
        wa_s[...] = wa_ref[...].astype(BF16)
        wb_s[...] = wb_ref[...].astype(BF16)
        wc_s[...] = wc_ref[...].astype(BF16)

    a = jnp.dot(ya_ref[...], wa_s[...], preferred_element_type=F32)
    m = ga_ref[...].astype(F32) * a
    b = jnp.dot(yb_ref[...], wb_s[...], preferred_element_type=F32)
    m = m + gb_ref[...].astype(F32) * b
    c = jnp.dot(at_ref[...], wc_s[...], preferred_element_type=F32)
    m = m + gc_ref[...].astype(F32) * c
    o_ref[...] = m.astype(o_ref.dtype)


def _merge(ya, yb, attn, gates, w_pa, w_pb, w_pc, layer):
    t = ya.shape[0]
    tm = 1024
    tn = 512
    nb = D // tn
    return pl.pallas_call(
        _merge_kernel,
        out_shape=jax.ShapeDtypeStruct((t, D), BF16),
        grid=(nb, t // tm),
        in_specs=[
            pl.BlockSpec((tm, GW), lambda j, i: (i, 0)),
            pl.BlockSpec((tm, CW), lambda j, i: (i, 0)),
            pl.BlockSpec((tm, D), lambda j, i: (i, 0)),
            pl.BlockSpec((None, GW, tn), lambda j, i: (layer, 0, j)),
            pl.BlockSpec((None, CW, tn), lambda j, i: (layer, 0, j)),
            pl.BlockSpec((None, D, tn), lambda j, i: (layer, 0, j)),
            pl.BlockSpec((tm, tn), lambda j, i: (i, j)),
            pl.BlockSpec((tm, tn), lambda j, i: (i, nb + j)),
            pl.BlockSpec((tm, tn), lambda j, i: (i, 2 * nb + j)),
        ],
        out_specs=pl.BlockSpec((tm, tn), lambda j, i: (i, j)),
        scratch_shapes=[pltpu.VMEM((GW, tn), BF16), pltpu.VMEM((CW, tn), BF16),
                        pltpu.VMEM((D, tn), BF16)],
        compiler_params=_cp(("arbitrary", "arbitrary"), 56),
        name="merge",
    )(ya, yb, attn, w_pa, w_pb, w_pc, gates, gates, gates)


def _outproj_kernel(m_ref, w_ref, x_ref, gt_ref, g_ref, sh_ref, sc_ref, x1_ref, h_ref, w_s):
    @pl.when(pl.program_id(0) == 0)
    def _():
        w_s[...] = w_ref[...].astype(BF16)

    y = jnp.dot(m_ref[...], w_s[...], preferred_element_type=F32)
    x1 = x_ref[...] + gt_ref[...] * y
    x1_ref[...] = x1
    h_ref[...] = (_rms(x1, g_ref[...]) * (1.0 + sc_ref[...]) + sh_ref[...]).astype(h_ref.dtype)


def _outproj(m, w_o, x, g_ffn, mods5, layer, ctx):
    t = m.shape[0]
    tm = 512
    row_of = _row_of(ctx, SEQ // tm, 0)
    return pl.pallas_call(
        _outproj_kernel,
        out_shape=(jax.ShapeDtypeStruct((t, D), F32), jax.ShapeDtypeStruct((t, D), BF16)),
        grid=(t // tm,),
        in_specs=[
            pl.BlockSpec((tm, D), lambda i: (i, 0)),
            pl.BlockSpec((None, D, D), lambda i: (layer, 0, 0), pipeline_mode=pl.Buffered(1)),
            pl.BlockSpec((tm, D), lambda i: (i, 0)),
            _mod_spec(layer, 2, row_of),
            pl.BlockSpec((1, D), lambda i: (0, 0)),
            _mod_spec(layer, 3, row_of),
            _mod_spec(layer, 4, row_of),
        ],
        out_specs=(pl.BlockSpec((tm, D), lambda i: (i, 0)),
                   pl.BlockSpec((tm, D), lambda i: (i, 0))),
        scratch_shapes=[pltpu.VMEM((D, D), BF16)],
        compiler_params=_cp(("arbitrary",), 56),
        name="out_proj",
    )(m, w_o, x, mods5, g_ffn.reshape(1, D), mods5, mods5)


FF_BLOCKS = D_FF // LANES
FF_SUBW = 2 * LANES
FF_NSUB = 2
TN_FF = FF_NSUB * FF_SUBW
FF_NBLK = TN_FF // LANES
FF_MCHUNK = 256


def _chunked_dot(h_ref, w):
    parts = [jnp.dot(h_ref[r0:r0 + FF_MCHUNK, :], w, preferred_element_type=F32)
             for r0 in range(0, h_ref.shape[0], FF_MCHUNK)]
    return jnp.concatenate(parts, axis=0)


def _ffn_up_kernel(h_ref, *refs, seq_len):
    a_refs = refs[0:FF_NBLK]
    b_refs = refs[FF_NBLK:2 * FF_NBLK]
    ca_refs = refs[2 * FF_NBLK:3 * FF_NBLK]
    cb_refs = refs[3 * FF_NBLK:4 * FF_NBLK]
    o_ref = refs[4 * FF_NBLK]
    ws = []
    for s in range(FF_NSUB):
        k0 = s * (FF_SUBW // LANES)
        k1 = k0 + FF_SUBW // LANES
        ws.append((jnp.concatenate([r[...] for r in a_refs[k0:k1]], axis=1).astype(BF16),
                   jnp.concatenate([r[...] for r in b_refs[k0:k1]], axis=1).astype(BF16),
                   jnp.concatenate([r[...] for r in ca_refs[k0:k1]], axis=1),
                   jnp.concatenate([r[...] for r in cb_refs[k0:k1]], axis=1)))
    for s in range(FF_NSUB):
        wa, wb, ca, cb = ws[s]
        a = _dwconv3(_chunked_dot(h_ref, wa), ca, seq_len)
        b = _dwconv3(_chunked_dot(h_ref, wb), cb, seq_len)
        o_ref[:, s * FF_SUBW:(s + 1) * FF_SUBW] = (a * _sigmoid(a) * b).astype(o_ref.dtype)


def _ffn_up(h, w_up, w_conv, layer, seq_len, rows):
    t = h.shape[0]
    nj = pl.cdiv(D_FF, TN_FF)
    last = 2 * FF_BLOCKS - 1

    def wspec(rows_, off, k):
        return pl.BlockSpec((None, rows_, LANES),
                            lambda i, j: (layer, 0, jnp.minimum(off + FF_NBLK * j + k, last)))

    blocks = range(FF_NBLK)
    specs = ([wspec(D, 0, k) for k in blocks] + [wspec(D, FF_BLOCKS, k) for k in blocks]
             + [wspec(3, 0, k) for k in blocks] + [wspec(3, FF_BLOCKS, k) for k in blocks])
    return pl.pallas_call(
        functools.partial(_ffn_up_kernel, seq_len=seq_len),
        out_shape=jax.ShapeDtypeStruct((t, D_FF), BF16),
        grid=(t // rows, nj),
        in_specs=[pl.BlockSpec((rows, D), lambda i, j: (i, 0))] + specs,
        out_specs=pl.BlockSpec((rows, TN_FF), lambda i, j: (i, j)),
        compiler_params=_cp(("arbitrary", "arbitrary"), 56),
        name="ffn_up",
    )(h, *([w_up] * (2 * FF_NBLK)), *([w_conv] * (2 * FF_NBLK)))


def _ffn_down_kernel(g_ref, w_ref, x_ref, gt_ref, o_ref, w_s):
    @pl.when(pl.program_id(1) == 0)
    def _():
        w_s[...] = w_ref[...].astype(BF16)

    y = jnp.dot(g_ref[...], w_s[...], preferred_element_type=F32)
    o_ref[...] = x_ref[...] + gt_ref[...] * y


def _ffn_down(g, w_down, x, mods5, layer, ctx):
    t = g.shape[0]
    tm = 512
    tn = 512
    row_of = _row_of(ctx, SEQ // tm, 1)
    return pl.pallas_call(
        _ffn_down_kernel,
        out_shape=jax.ShapeDtypeStruct((t, D), F32),
        grid=(D // tn, t // tm),
        in_specs=[
            pl.BlockSpec((tm, D_FF), lambda j, i: (i, 0)),
            pl.BlockSpec((None, D_FF, tn), lambda j, i: (layer, 0, j)),
            pl.BlockSpec((tm, tn), lambda j, i: (i, j)),
            _mod_spec(layer, 5, row_of, tn, lambda j, i: j),
        ],
        out_specs=pl.BlockSpec((tm, tn), lambda j, i: (i, j)),
        scratch_shapes=[pltpu.VMEM((D_FF, tn), BF16)],
        compiler_params=_cp(("arbitrary", "arbitrary"), 54),
        name="ffn_down",
    )(g, w_down, x, mods5)


def kernel(x, c, ctx, c_ctx, w_ada, b_ada, g_mix, w_in, g_q, g_k, w_gmlp, b_gmlp, g_gmlp_v,
           w_sconv, w_pa, w_pb, w_pc, w_o, g_ffn, w_up, w_ffn_conv, w_down, g_final):
    cvec = jnp.concatenate([c, c_ctx[None, :], jnp.zeros((3, D), F32)], axis=0)
    mods5 = _mods(cvec, w_ada, b_ada).reshape(DEPTH, 8, 6, 1, D)
    rope_tabs = _rope_tables()
    q_scale = HD ** -0.5 * math.log2(math.e)

    xs = x.reshape(BATCH * SEQ, D)
    cs = ctx.reshape(BATCH * CTX, D)

    for l in range(DEPTH):
        last = l == DEPTH - 1
        bs = jnp.broadcast_to(b_gmlp[l][:, :, None], (GG, CHUNK, CHUNK))
        hx = _normmod(xs, g_mix[l], mods5, l, False)
        hc = _normmod(cs, g_mix[l], mods5, l, True)

        raw_x = _inproj(hx, w_in, l, COLS_RAW, "raw")
        zg_x = _inproj(hx, w_in, l, COLS_GELU, "gelu")
        gates_x = _inproj(hx, w_in, l, COLS_GATE, "sigmoid")
        if not last:
            raw_c = _inproj(hc, w_in, l, COLS_RAW, "raw")
            kc = _headnorm(raw_c, R_K // KV_DIM, N_KV, g_k[l], 1.0, None)
            ctx_kv = (kc, raw_c, R_V // HD)
        else:
            kv_c = _inproj(hc, w_in, l, COLS_KV, "raw")
            kc = _headnorm(kv_c, 0, N_KV, g_k[l], 1.0, None)
            ctx_kv = (kc, kv_c, KV_DIM // HD)
        qx = _headnorm(raw_x, 0, N_HEADS, g_q[l], q_scale, rope_tabs)
        kx = _headnorm(raw_x, R_K // KV_DIM, N_KV, g_k[l], 1.0, rope_tabs)
        attn_x = _attention(qx, SEQ, (kx, raw_x, R_V // HD), ctx_kv)
        ya = _gmlp(zg_x, g_gmlp_v[l], w_gmlp[l], bs)
        yb = _sconv(raw_x, w_sconv[l], SEQ, SEQ)
        m = _merge(ya, yb, attn_x, gates_x, w_pa, w_pb, w_pc, l)
        x1, fx = _outproj(m, w_o, xs, g_ffn[l], mods5, l, False)

        gx = _ffn_up(fx, w_up, w_ffn_conv, l, SEQ, SEQ)
        xs = _ffn_down(gx, w_down, x1, mods5, l, False)

        if not last:
            zg_c = _inproj(hc, w_in, l, COLS_GELU, "gelu")
            gates_c = _inproj(hc, w_in, l, COLS_GATE, "sigmoid")
            qc = _headnorm(raw_c, 0, N_HEADS, g_q[l], q_scale, None)
            attn_c = _attention(qc, CTX, None, ctx_kv)
            yac = _gmlp(zg_c, g_gmlp_v[l], w_gmlp[l], bs)
            ybc = _sconv(raw_c, w_sconv[l], CTX, BATCH * CTX)
            mc = _merge(yac, ybc, attn_c, gates_c, w_pa, w_pb, w_pc, l)
            c1, fc = _outproj(mc, w_o, cs, g_ffn[l], mods5, l, True)
            gc = _ffn_up(fc, w_up, w_ffn_conv, l, CTX, BATCH * CTX)
            cs = _ffn_down(gc, w_down, c1, mods5, l, True)

    return _final_norm(xs, g_final).reshape(BATCH, SEQ, D)
```

```python
import functools
import math

import jax
import jax.numpy as jnp
from jax.experimental import pallas as pl
from jax.experimental.pallas import tpu as pltpu

F32 = jnp.float32
BF16 = jnp.bfloat16

D = 2048
BATCH = 4
SEQ = 2048
CTX = 256
DEPTH = 2
GRID_W = 64
N_HEADS = 16
N_KV = 4
HD = 128
GROUP = N_HEADS // N_KV
CHUNK = 128
GW = D // 2
GG = 8
CW = D // 2
D_FF = 5504
EPS = 1e-6
ROPE_THETA = 10000.0
KV_DIM = N_KV * HD
IN_DIM = D + 2 * KV_DIM + 2 * GW + 3 * CW + 3 * D
OFF_K = D
OFF_U = D + 2 * KV_DIM
OFF_B = OFF_U + 2 * GW
OFF_G = OFF_B + 3 * CW
R_K = D
R_V = D + KV_DIM
R_B = D + 2 * KV_DIM
R_C = R_B + CW
R_H = R_C + CW
LANES = 128

VMEM_PHYS_V7X = 64 * 1024 * 1024
VMEM_CAP = VMEM_PHYS_V7X - 8 * 1024 * 1024


def _cp(sem, vmem_mb):
    return pltpu.CompilerParams(
        dimension_semantics=sem,
        vmem_limit_bytes=min(int(vmem_mb * 1024 * 1024), VMEM_CAP))


def _rms(xf, g):
    ms = jnp.mean(xf * xf, axis=-1, keepdims=True)
    return xf * jax.lax.rsqrt(ms + EPS) * g


def _sigmoid(x):
    return 0.5 * jnp.tanh(0.5 * x) + 0.5


def _gelu_tanh(x):
    c = math.sqrt(2.0 / math.pi)
    return 0.5 * x * (1.0 + jnp.tanh(c * (x + 0.044715 * (x * x * x))))


def _mods_kernel(c_ref, w_ref, b_ref, o_ref):
    c = c_ref[...]
    s = (c * _sigmoid(c)).astype(BF16)
    w = w_ref[...].astype(BF16)
    o_ref[...] = jnp.dot(s, w, preferred_element_type=F32) + b_ref[...]


def _mods(cvec, w_ada, b_ada):
    tn = 1024
    nj = 6 * D // tn
    return pl.pallas_call(
        _mods_kernel,
        out_shape=jax.ShapeDtypeStruct((DEPTH, 8, 6 * D), F32),
        grid=(DEPTH, nj),
        in_specs=[
            pl.BlockSpec((8, D), lambda l, j: (0, 0)),
            pl.BlockSpec((None, D, tn), lambda l, j: (l, 0, j)),
            pl.BlockSpec((None, 1, tn), lambda l, j: (l, 0, j)),
        ],
        out_specs=pl.BlockSpec((None, 8, tn), lambda l, j: (l, 0, j)),
        compiler_params=_cp(("arbitrary", "arbitrary"), 40),
        name="adaln_mods",
    )(cvec, w_ada, b_ada.reshape(DEPTH, 1, 6 * D))


def _mod_spec(layer, k, row_of, width=D, col_of=None):
    if col_of is None:
        return pl.BlockSpec((None, None, None, 1, width),
                            lambda *g: (layer, row_of(*g), k, 0, 0))
    return pl.BlockSpec((None, None, None, 1, width),
                        lambda *g: (layer, row_of(*g), k, 0, col_of(*g)))


def _row_of(ctx, tiles_per_batch, axis):
    if ctx:
        return lambda *g: BATCH
    return lambda *g: g[axis] // tiles_per_batch


def _normmod_kernel(x_ref, g_ref, sh_ref, sc_ref, o_ref):
    y = _rms(x_ref[...], g_ref[...])
    o_ref[...] = (y * (1.0 + sc_ref[...]) + sh_ref[...]).astype(o_ref.dtype)


def _norm_kernel(x_ref, g_ref, o_ref):
    o_ref[...] = _rms(x_ref[...], g_ref[...]).astype(o_ref.dtype)


def _normmod(x, g, mods5, layer, ctx):
    t = x.shape[0]
    tm = 1024
    row_of = _row_of(ctx, SEQ // tm, 0)
    return pl.pallas_call(
        _normmod_kernel,
        out_shape=jax.ShapeDtypeStruct((t, D), BF16),
        grid=(t // tm,),
        in_specs=[
            pl.BlockSpec((tm, D), lambda i: (i, 0)),
            pl.BlockSpec((1, D), lambda i: (0, 0)),
            _mod_spec(layer, 0, row_of),
            _mod_spec(layer, 1, row_of),
        ],
        out_specs=pl.BlockSpec((tm, D), lambda i: (i, 0)),
        compiler_params=_cp(("arbitrary",), 44),
        name="norm_mod",
    )(x, g.reshape(1, D), mods5, mods5)


def _final_norm(x, g):
    t = x.shape[0]
    tm = 1024
    return pl.pallas_call(
        _norm_kernel,
        out_shape=jax.ShapeDtypeStruct((t, D), F32),
        grid=(t // tm,),
        in_specs=[
            pl.BlockSpec((tm, D), lambda i: (i, 0)),
            pl.BlockSpec((1, D), lambda i: (0, 0)),
        ],
        out_specs=pl.BlockSpec((tm, D), lambda i: (i, 0)),
        compiler_params=_cp(("arbitrary",), 48),
        name="final_norm",
    )(x, g.reshape(1, D))


TN_IN = 1024
IN_MCHUNK = {"raw": 1024, "sigmoid": 1024, "gelu": 512}
_ACTS = {"raw": lambda a: a, "gelu": _gelu_tanh, "sigmoid": _sigmoid}


def _inproj_kernel(h_ref, w_ref, o_ref, wb_ref, *, act):
    @pl.when(pl.program_id(1) == 0)
    def _():
        wb_ref[...] = w_ref[...].astype(BF16)

    w = wb_ref[...]
    mc = min(IN_MCHUNK[act], h_ref.shape[0])
    for r0 in range(0, h_ref.shape[0], mc):
        acc = jnp.dot(h_ref[r0:r0 + mc, :], w, preferred_element_type=F32)
        o_ref[r0:r0 + mc, :] = _ACTS[act](acc).astype(o_ref.dtype)


def _inproj(h, w, layer, col_tiles, act):
    t = h.shape[0]
    tm = min(t, 2048)
    first, nj, skip_at, skip = col_tiles

    def wmap(j, i):
        return (layer, 0, first + j + skip * (j // skip_at))

    return pl.pallas_call(
        functools.partial(_inproj_kernel, act=act),
        out_shape=jax.ShapeDtypeStruct((t, nj * TN_IN), BF16),
        grid=(nj, t // tm),
        in_specs=[
            pl.BlockSpec((tm, D), lambda j, i: (i, 0)),
            pl.BlockSpec((None, D, TN_IN), wmap),
        ],
        out_specs=pl.BlockSpec((tm, TN_IN), lambda j, i: (i, j)),
        scratch_shapes=[pltpu.VMEM((D, TN_IN), BF16)],
        compiler_params=_cp(("arbitrary", "arbitrary"), 56),
        name="in_proj_" + act,
    )(h, w)


COLS_RAW = (0, 6, 3, (OFF_B - OFF_U) // TN_IN)
COLS_GELU = (OFF_U // TN_IN, (OFF_B - OFF_U) // TN_IN, 1 << 20, 0)
COLS_GATE = (OFF_G // TN_IN, 3 * D // TN_IN, 1 << 20, 0)
COLS_KV = (OFF_K // TN_IN, 1, 1 << 20, 0)


def _rope_tables():
    rows = SEQ // GRID_W
    row = jnp.repeat(jnp.arange(rows, dtype=F32), GRID_W)
    col = jnp.tile(jnp.arange(GRID_W, dtype=F32), rows)
    n_freq = HD // 4
    inv_freq = ROPE_THETA ** (-jnp.arange(n_freq, dtype=F32) / n_freq)
    ar = row[:, None] * inv_freq
    ac = col[:, None] * inv_freq
    cos = jnp.concatenate([jnp.cos(ar), jnp.cos(ar), jnp.cos(ac), jnp.cos(ac)], axis=-1)
    sin = jnp.concatenate([-jnp.sin(ar), jnp.sin(ar), -jnp.sin(ac), jnp.sin(ac)], axis=-1)
    return cos, sin


def _headnorm_kernel(x_ref, g_ref, *rest, n_heads, scale, rope):
    def first_half_mask(rows):
        lane = jax.lax.broadcasted_iota(jnp.int32, (rows, HD), 1)
        return (lane % (HD // 2)) < (HD // 4)

    def swap_halves(v, mask):
        return jnp.where(mask, pltpu.roll(v, HD - HD // 4, axis=1), pltpu.roll(v, HD // 4, axis=1))

    g = g_ref[...] * scale
    if rope:
        cos_ref, sin_ref, o_ref = rest
        g_partner = swap_halves(jnp.broadcast_to(g, (8, HD)), first_half_mask(8))[0:1, :]
        gcos = cos_ref[...] * g
        gsin = sin_ref[...] * g_partner
        packed_mask = first_half_mask(x_ref.shape[0] // 2)
    else:
        (o_ref,) = rest
    for h in range(n_heads):
        xb = x_ref[:, h * HD:(h + 1) * HD]
        x = xb.astype(F32)
        if rope:
            rstd = jax.lax.rsqrt(jnp.mean(x * x, axis=-1, keepdims=True) + EPS)
            packed = pltpu.bitcast(xb, jnp.uint32)
            partner = pltpu.bitcast(swap_halves(packed, packed_mask), BF16).astype(F32)
            y = (x * gcos + partner * gsin) * rstd
        else:
            y = _rms(x, g)
        o_ref[:, h * HD:(h + 1) * HD] = y.astype(o_ref.dtype)


def _headnorm(src, col_blk, n_heads, g, scale, rope_tabs):
    t = src.shape[0]
    tm = 1024
    w = n_heads * HD
    rope = rope_tabs is not None
    in_specs = [
        pl.BlockSpec((tm, w), lambda i: (i, col_blk)),
        pl.BlockSpec((1, HD), lambda i: (0, 0)),
    ]
    args = [src, g.reshape(1, HD)]
    if rope:
        nt = SEQ // tm
        in_specs += [pl.BlockSpec((tm, HD), lambda i: (i % nt, 0))] * 2
        args += list(rope_tabs)
    return pl.pallas_call(
        functools.partial(_headnorm_kernel, n_heads=n_heads, scale=scale, rope=rope),
        out_shape=jax.ShapeDtypeStruct((t, w), BF16),
        grid=(t // tm,),
        in_specs=in_specs,
        out_specs=pl.BlockSpec((tm, w), lambda i: (i, 0)),
        compiler_params=_cp(("arbitrary",), 44),
        name="head_norm",
    )(*args)


TQ_SUB = 256


KEY_TILE = 256


def _attn_kernel(q_ref, *refs, with_latent):
    if with_latent:
        kx_ref, vx_ref, kc_ref, vc_ref, o_ref, k_s, vt_s = refs
        k_s[0:SEQ, :] = kx_ref[...]
        k_s[SEQ:SEQ + CTX, :] = kc_ref[...]
        vt_s[:, 0:SEQ] = vx_ref[...].T
        vt_s[:, SEQ:SEQ + CTX] = vc_ref[...].T
    else:
        kc_ref, vc_ref, o_ref, k_s, vt_s = refs
        k_s[...] = kc_ref[...]
        vt_s[...] = vc_ref[...].T
    tiles = list(range(0, k_s.shape[0], KEY_TILE))
    units = [(r0, g0) for r0 in range(0, q_ref.shape[0], TQ_SUB) for g0 in range(0, GROUP, 2)]

    def queries(u):
        r0, g0 = units[u]
        return jnp.concatenate([q_ref[r0:r0 + TQ_SUB, g0 * HD:(g0 + 1) * HD],
                                q_ref[r0:r0 + TQ_SUB, (g0 + 1) * HD:(g0 + 2) * HD]], axis=0)

    def scores(q, t0):
        return jax.lax.dot_general(k_s[t0:t0 + KEY_TILE, :], q, (((1,), (1,)), ((), ())),
                                   preferred_element_type=F32)

    def colmax(sts):
        m = None
        for st in sts:
            mc = jnp.max(st, axis=0, keepdims=True)
            m = mc if m is None else jnp.maximum(m, mc)
        return m

    sts = [scores(queries(0), t0) for t0 in tiles]
    m = colmax(sts)
    for u, (r0, g0) in enumerate(units):
        q_next = queries(u + 1) if u + 1 < len(units) else None
        sts_next = []
        l = None
        ot = None
        for t0, st in zip(tiles, sts):
            p = jnp.exp2(st - m)
            lc = jnp.sum(p, axis=0, keepdims=True)
            oc = jnp.dot(vt_s[:, t0:t0 + KEY_TILE], p.astype(BF16),
                         preferred_element_type=F32)
            l = lc if l is None else l + lc
            ot = oc if ot is None else ot + oc
            if q_next is not None:
                sts_next.append(scores(q_next, t0))
        ot = ot * (1.0 / l)
        o_ref[r0:r0 + TQ_SUB, g0 * HD:(g0 + 1) * HD] = ot[:, :TQ_SUB].T.astype(o_ref.dtype)
        o_ref[r0:r0 + TQ_SUB, (g0 + 1) * HD:(g0 + 2) * HD] = ot[:, TQ_SUB:].T.astype(o_ref.dtype)
        if q_next is not None:
            sts = sts_next
            m = colmax(sts)


def _attention(q, lq, latent, context):
    with_latent = latent is not None
    s_len = (SEQ if with_latent else 0) + CTX
    tq = min(lq, 1024)
    nq = lq // tq
    in_specs = [pl.BlockSpec((tq, GROUP * HD), lambda b, h, i: (b * nq + i, h))]
    args = [q]
    for src, rows in ((latent, SEQ), (context, CTX)):
        if src is None:
            continue
        k_arr, v_arr, v_blk = src
        in_specs += [pl.BlockSpec((rows, HD), lambda b, h, i: (b, h)),
                     pl.BlockSpec((rows, HD), lambda b, h, i, v_blk=v_blk: (b, v_blk + h))]
        args += [k_arr, v_arr]
    return pl.pallas_call(
        functools.partial(_attn_kernel, with_latent=with_latent),
        out_shape=jax.ShapeDtypeStruct((BATCH * lq, D), BF16),
        grid=(BATCH, N_KV, nq),
        in_specs=in_specs,
        out_specs=pl.BlockSpec((tq, GROUP * HD), lambda b, h, i: (b * nq + i, h)),
        scratch_shapes=[pltpu.VMEM((s_len, HD), BF16), pltpu.VMEM((HD, s_len), BF16)],
        compiler_params=_cp(("arbitrary", "arbitrary", "arbitrary"), 48),
        name="attention",
    )(*args)


def _gmlp_kernel(u_ref, v_ref, gv_ref, ws_ref, bs_ref, o_ref, *, tm):
    v = _rms(v_ref[...].astype(F32), gv_ref[...]).astype(BF16)
    for g in range(GG):
        c0 = g * CHUNK
        w = ws_ref[g].astype(BF16)
        for n in range(tm // CHUNK):
            r0 = n * CHUNK
            mixed = jnp.dot(w, v[r0:r0 + CHUNK, c0:c0 + CHUNK],
                            preferred_element_type=F32) + bs_ref[g]
            u = u_ref[r0:r0 + CHUNK, c0:c0 + CHUNK].astype(F32)
            o_ref[r0:r0 + CHUNK, c0:c0 + CHUNK] = (u * mixed).astype(o_ref.dtype)


def _gmlp(zg, gv, ws, bs):
    t = zg.shape[0]
    tm = 512
    return pl.pallas_call(
        functools.partial(_gmlp_kernel, tm=tm),
        out_shape=jax.ShapeDtypeStruct((t, GW), BF16),
        grid=(t // tm,),
        in_specs=[
            pl.BlockSpec((tm, GW), lambda i: (i, 0)),
            pl.BlockSpec((tm, GW), lambda i: (i, 1)),
            pl.BlockSpec((1, GW), lambda i: (0, 0)),
            pl.BlockSpec((GG, CHUNK, CHUNK), lambda i: (0, 0, 0)),
            pl.BlockSpec((GG, CHUNK, CHUNK), lambda i: (0, 0, 0)),
        ],
        out_specs=pl.BlockSpec((tm, GW), lambda i: (i, 0)),
        compiler_params=_cp(("arbitrary",), 32),
        name="gmlp",
    )(zg, zg, gv.reshape(1, GW), ws, bs)


def _dwconv3(x, w, seq_len):
    rows = x.shape[0]
    pos = jax.lax.broadcasted_iota(jnp.int32, x.shape, 0) % seq_len
    prev = jnp.where(pos == 0, 0.0, pltpu.roll(x, 1, axis=0))
    nxt = jnp.where(pos == seq_len - 1, 0.0, pltpu.roll(x, rows - 1, axis=0))
    return prev * w[0:1, :] + x * w[1:2, :] + nxt * w[2:3, :]


def _sconv_kernel(b_ref, c_ref, h_ref, w_ref, o_ref, *, seq_len):
    t = c_ref[...].astype(F32) * h_ref[...].astype(F32)
    y = b_ref[...].astype(F32) * _dwconv3(t, w_ref[...], seq_len)
    o_ref[...] = y.astype(o_ref.dtype)


def _sconv(raw, w, seq_len, rows):
    t = raw.shape[0]
    tc = 256
    return pl.pallas_call(
        functools.partial(_sconv_kernel, seq_len=seq_len),
        out_shape=jax.ShapeDtypeStruct((t, CW), BF16),
        grid=(t // rows, CW // tc),
        in_specs=[
            pl.BlockSpec((rows, tc), lambda i, j: (i, R_B // tc + j)),
            pl.BlockSpec((rows, tc), lambda i, j: (i, R_C // tc + j)),
            pl.BlockSpec((rows, tc), lambda i, j: (i, R_H // tc + j)),
            pl.BlockSpec((3, tc), lambda i, j: (0, j)),
        ],
        out_specs=pl.BlockSpec((rows, tc), lambda i, j: (i, j)),
        compiler_params=_cp(("arbitrary", "arbitrary"), 40),
        name="short_conv",
    )(raw, raw, raw, w)


def _merge_kernel(ya_ref, yb_ref, at_ref, wa_ref, wb_ref, wc_ref,
                  ga_ref, gb_ref, gc_ref, o_ref, wa_s, wb_s, wc_s):
    @pl.when(pl.program_id(1) == 0)
    def _():
        wa_s[...] = wa_ref[...].astype(BF16)
        wb_s[...] = wb_ref[...].astype(BF16)
        wc_s[...] = wc_ref[...].astype(BF16)

    a = jnp.dot(ya_ref[...], wa_s[...], preferred_element_type=F32)
    m = ga_ref[...].astype(F32) * a
    b = jnp.dot(yb_ref[...], wb_s[...], preferred_element_type=F32)
    m = m + gb_ref[...].astype(F32) * b
    c = jnp.dot(at_ref[...], wc_s[...], preferred_element_type=F32)
    m = m + gc_ref[...].astype(F32) * c
    o_ref[...] = m.astype(o_ref.dtype)


def _merge(ya, yb, attn, gates, w_pa, w_pb, w_pc, layer):
    t = ya.shape[0]
    tm = 1024
    tn = 512
    nb = D // tn
    return pl.pallas_call(
        _merge_kernel,
        out_shape=jax.ShapeDtypeStruct((t, D), BF16),
        grid=(nb, t // tm),
        in_specs=[
            pl.BlockSpec((tm, GW), lambda j, i: (i, 0)),
            pl.BlockSpec((tm, CW), lambda j, i: (i, 0)),
            pl.BlockSpec((tm, D), lambda j, i: (i, 0)),
            pl.BlockSpec((None, GW, tn), lambda j, i: (layer, 0, j)),
            pl.BlockSpec((None, CW, tn), lambda j, i: (layer, 0, j)),
            pl.BlockSpec((None, D, tn), lambda j, i: (layer, 0, j)),
            pl.BlockSpec((tm, tn), lambda j, i: (i, j)),
            pl.BlockSpec((tm, tn), lambda j, i: (i, nb + j)),
            pl.BlockSpec((tm, tn), lambda j, i: (i, 2 * nb + j)),
        ],
        out_specs=pl.BlockSpec((tm, tn), lambda j, i: (i, j)),
        scratch_shapes=[pltpu.VMEM((GW, tn), BF16), pltpu.VMEM((CW, tn), BF16),
                        pltpu.VMEM((D, tn), BF16)],
        compiler_params=_cp(("arbitrary", "arbitrary"), 56),
        name="merge",
    )(ya, yb, attn, w_pa, w_pb, w_pc, gates, gates, gates)


def _outproj_kernel(m_ref, w_ref, x_ref, gt_ref, g_ref, sh_ref, sc_ref, x1_ref, h_ref, w_s):
    @pl.when(pl.program_id(0) == 0)
    def _():
        w_s[...] = w_ref[...].astype(BF16)

    y = jnp.dot(m_ref[...], w_s[...], preferred_element_type=F32)
    x1 = x_ref[...] + gt_ref[...] * y
    x1_ref[...] = x1
    h_ref[...] = (_rms(x1, g_ref[...]) * (1.0 + sc_ref[...]) + sh_ref[...]).astype(h_ref.dtype)


def _outproj(m, w_o, x, g_ffn, mods5, layer, ctx):
    t = m.shape[0]
    tm = 512
    row_of = _row_of(ctx, SEQ // tm, 0)
    return pl.pallas_call(
        _outproj_kernel,
        out_shape=(jax.ShapeDtypeStruct((t, D), F32), jax.ShapeDtypeStruct((t, D), BF16)),
        grid=(t // tm,),
        in_specs=[
            pl.BlockSpec((tm, D), lambda i: (i, 0)),
            pl.BlockSpec((None, D, D), lambda i: (layer, 0, 0), pipeline_mode=pl.Buffered(1)),
            pl.BlockSpec((tm, D), lambda i: (i, 0)),
            _mod_spec(layer, 2, row_of),
            pl.BlockSpec((1, D), lambda i: (0, 0)),
            _mod_spec(layer, 3, row_of),
            _mod_spec(layer, 4, row_of),
        ],
        out_specs=(pl.BlockSpec((tm, D), lambda i: (i, 0)),
                   pl.BlockSpec((tm, D), lambda i: (i, 0))),
        scratch_shapes=[pltpu.VMEM((D, D), BF16)],
        compiler_params=_cp(("arbitrary",), 56),
        name="out_proj",
    )(m, w_o, x, mods5, g_ffn.reshape(1, D), mods5, mods5)


FF_BLOCKS = D_FF // LANES
FF_SUBW = 2 * LANES
FF_NSUB = 2
TN_FF = FF_NSUB * FF_SUBW
FF_NBLK = TN_FF // LANES
FF_MCHUNK = 256


def _chunked_dot(h_ref, w):
    parts = [jnp.dot(h_ref[r0:r0 + FF_MCHUNK, :], w, preferred_element_type=F32)
             for r0 in range(0, h_ref.shape[0], FF_MCHUNK)]
    return jnp.concatenate(parts, axis=0)


def _ffn_up_kernel(h_ref, *refs, seq_len):
    a_refs = refs[0:FF_NBLK]
    b_refs = refs[FF_NBLK:2 * FF_NBLK]
    ca_refs = refs[2 * FF_NBLK:3 * FF_NBLK]
    cb_refs = refs[3 * FF_NBLK:4 * FF_NBLK]
    o_ref = refs[4 * FF_NBLK]
    ws = []
    for s in range(FF_NSUB):
        k0 = s * (FF_SUBW // LANES)
        k1 = k0 + FF_SUBW // LANES
        ws.append((jnp.concatenate([r[...] for r in a_refs[k0:k1]], axis=1).astype(BF16),
                   jnp.concatenate([r[...] for r in b_refs[k0:k1]], axis=1).astype(BF16),
                   jnp.concatenate([r[...] for r in ca_refs[k0:k1]], axis=1),
                   jnp.concatenate([r[...] for r in cb_refs[k0:k1]], axis=1)))
    for s in range(FF_NSUB):
        wa, wb, ca, cb = ws[s]
        a = _dwconv3(_chunked_dot(h_ref, wa), ca, seq_len)
        b = _dwconv3(_chunked_dot(h_ref, wb), cb, seq_len)
        o_ref[:, s * FF_SUBW:(s + 1) * FF_SUBW] = (a * _sigmoid(a) * b).astype(o_ref.dtype)


def _ffn_up(h, w_up, w_conv, layer, seq_len, rows):
    t = h.shape[0]
    nj = pl.cdiv(D_FF, TN_FF)
    last = 2 * FF_BLOCKS - 1

    def wspec(rows_, off, k):
        return pl.BlockSpec((None, rows_, LANES),
                            lambda i, j: (layer, 0, jnp.minimum(off + FF_NBLK * j + k, last)))

    blocks = range(FF_NBLK)
    specs = ([wspec(D, 0, k) for k in blocks] + [wspec(D, FF_BLOCKS, k) for k in blocks]
             + [wspec(3, 0, k) for k in blocks] + [wspec(3, FF_BLOCKS, k) for k in blocks])
    return pl.pallas_call(
        functools.partial(_ffn_up_kernel, seq_len=seq_len),
        out_shape=jax.ShapeDtypeStruct((t, D_FF), BF16),
        grid=(t // rows, nj),
        in_specs=[pl.BlockSpec((rows, D), lambda i, j: (i, 0))] + specs,
        out_specs=pl.BlockSpec((rows, TN_FF), lambda i, j: (i, j)),
        compiler_params=_cp(("arbitrary", "arbitrary"), 56),
        name="ffn_up",
    )(h, *([w_up] * (2 * FF_NBLK)), *([w_conv] * (2 * FF_NBLK)))


def _ffn_down_kernel(g_ref, w_ref, x_ref, gt_ref, o_ref, w_s):
    @pl.when(pl.program_id(1) == 0)
    def _():
        w_s[...] = w_ref[...].astype(BF16)

    y = jnp.dot(g_ref[...], w_s[...], preferred_element_type=F32)
    o_ref[...] = x_ref[...] + gt_ref[...] * y


def _ffn_down(g, w_down, x, mods5, layer, ctx):
    t = g.shape[0]
    tm = 512
    tn = 512
    row_of = _row_of(ctx, SEQ // tm, 1)
    return pl.pallas_call(
        _ffn_down_kernel,
        out_shape=jax.ShapeDtypeStruct((t, D), F32),
        grid=(D // tn, t // tm),
        in_specs=[
            pl.BlockSpec((tm, D_FF), lambda j, i: (i, 0)),
            pl.BlockSpec((None, D_FF, tn), lambda j, i: (layer, 0, j)),
            pl.BlockSpec((tm, tn), lambda j, i: (i, j)),
            _mod_spec(layer, 5, row_of, tn, lambda j, i: j),
        ],
        out_specs=pl.BlockSpec((tm, tn), lambda j, i: (i, j)),
        scratch_shapes=[pltpu.VMEM((D_FF, tn), BF16)],
        compiler_params=_cp(("arbitrary", "arbitrary"), 54),
        name="ffn_down",
    )(g, w_down, x, mods5)


def kernel(x, c, ctx, c_ctx, w_ada, b_ada, g_mix, w_in, g_q, g_k, w_gmlp, b_gmlp, g_gmlp_v,
           w_sconv, w_pa, w_pb, w_pc, w_o, g_ffn, w_up, w_ffn_conv, w_down, g_final):
    cvec = jnp.concatenate([c, c_ctx[None, :], jnp.zeros((3, D), F32)], axis=0)
    mods5 = _mods(cvec, w_ada, b_ada).reshape(DEPTH, 8, 6, 1, D)
    rope_tabs = _rope_tables()
    q_scale = HD ** -0.5 * math.log2(math.e)

    xs = x.reshape(BATCH * SEQ, D)
    cs = ctx.reshape(BATCH * CTX, D)

    for l in range(DEPTH):
        last = l == DEPTH - 1
        bs = jnp.broadcast_to(b_gmlp[l][:, :, None], (GG, CHUNK, CHUNK))
        hx = _normmod(xs, g_mix[l], mods5, l, False)
        hc = _normmod(cs, g_mix[l], mods5, l, True)

        raw_x = _inproj(hx, w_in, l, COLS_RAW, "raw")
        zg_x = _inproj(hx, w_in, l, COLS_GELU, "gelu")
        gates_x = _inproj(hx, w_in, l, COLS_GATE, "sigmoid")
        if not last:
            raw_c = _inproj(hc, w_in, l, COLS_RAW, "raw")
            kc = _headnorm(raw_c, R_K // KV_DIM, N_KV, g_k[l], 1.0, None)
            ctx_kv = (kc, raw_c, R_V // HD)
        else:
            kv_c = _inproj(hc, w_in, l, COLS_KV, "raw")
            kc = _headnorm(kv_c, 0, N_KV, g_k[l], 1.0, None)
            ctx_kv = (kc, kv_c, KV_DIM // HD)
        qx = _headnorm(raw_x, 0, N_HEADS, g_q[l], q_scale, rope_tabs)
        kx = _headnorm(raw_x, R_K // KV_DIM, N_KV, g_k[l], 1.0, rope_tabs)
        attn_x = _attention(qx, SEQ, (kx, raw_x, R_V // HD), ctx_kv)
        ya = _gmlp(zg_x, g_gmlp_v[l], w_gmlp[l], bs)
        yb = _sconv(raw_x, w_sconv[l], SEQ, SEQ)
        m = _merge(ya, yb, attn_x, gates_x, w_pa, w_pb, w_pc, l)
        x1, fx = _outproj(m, w_o, xs, g_ffn[l], mods5, l, False)

        gx = _ffn_up(fx, w_up, w_ffn_conv, l, SEQ, SEQ)
        xs = _ffn_down(gx, w_down, x1, mods5, l, False)

        if not last:
            zg_c = _inproj(hc, w_in, l, COLS_GELU, "gelu")
            gates_c = _inproj(hc, w_in, l, COLS_GATE, "sigmoid")
            qc = _headnorm(raw_c, 0, N_HEADS, g_q[l], q_scale, None)
            attn_c = _attention(qc, CTX, None, ctx_kv)
            yac = _gmlp(zg_c, g_gmlp_v[l], w_gmlp[l], bs)
            ybc = _sconv(raw_c, w_sconv[l], CTX, BATCH * CTX)
            mc = _merge(yac, ybc, attn_c, gates_c, w_pa, w_pb, w_pc, l)
            c1, fc = _outproj(mc, w_o, cs, g_ffn[l], mods5, l, True)
            gc = _ffn_up(fc, w_up, w_ffn_conv, l, CTX, BATCH * CTX)
            cs = _ffn_down(gc, w_down, c1, mods5, l, True)

    return _final_norm(xs, g_final).reshape(BATCH, SEQ, D)
```

```python
import functools
import math

import jax
import jax.numpy as jnp
from jax.experimental import pallas as pl
from jax.experimental.pallas import tpu as pltpu

F32 = jnp.float32
BF16 = jnp.bfloat16

D = 2048
BATCH = 4
SEQ = 2048
CTX = 256
DEPTH = 2
GRID_W = 64
N_HEADS = 16
N_KV = 4
HD = 128
GROUP = N_HEADS // N_KV
CHUNK = 128
GW = D // 2
GG = 8
CW = D // 2
D_FF = 5504
EPS = 1e-6
ROPE_THETA = 10000.0
KV_DIM = N_KV * HD
IN_DIM = D + 2 * KV_DIM + 2 * GW + 3 * CW + 3 * D
OFF_K = D
OFF_U = D + 2 * KV_DIM
OFF_B = OFF_U + 2 * GW
OFF_G = OFF_B + 3 * CW
R_K = D
R_V = D + KV_DIM
R_B = D + 2 * KV_DIM
R_C = R_B + CW
R_H = R_C + CW
LANES = 128

VMEM_PHYS_V7X = 64 * 1024 * 1024
VMEM_CAP = VMEM_PHYS_V7X - 8 * 1024 * 1024


def _cp(sem, vmem_mb):
    return pltpu.CompilerParams(
        dimension_semantics=sem,
        vmem_limit_bytes=min(int(vmem_mb * 1024 * 1024), VMEM_CAP))


def _rms(xf, g):
    ms = jnp.mean(xf * xf, axis=-1, keepdims=True)
    return xf * jax.lax.rsqrt(ms + EPS) * g


def _sigmoid(x):
    return 0.5 * jnp.tanh(0.5 * x) + 0.5


def _gelu_tanh(x):
    c = math.sqrt(2.0 / math.pi)
    return 0.5 * x * (1.0 + jnp.tanh(c * (x + 0.044715 * (x * x * x))))


def _mods_kernel(c_ref, w_ref, b_ref, o_ref):
    c = c_ref[...]
    s = (c * _sigmoid(c)).astype(BF16)
    w = w_ref[...].astype(BF16)
    o_ref[...] = jnp.dot(s, w, preferred_element_type=F32) + b_ref[...]


def _mods(cvec, w_ada, b_ada):
    tn = 2048
    nj = 6 * D // tn
    return pl.pallas_call(
        _mods_kernel,
        out_shape=jax.ShapeDtypeStruct((DEPTH, 8, 6 * D), F32),
        grid=(DEPTH, nj),
        in_specs=[
            pl.BlockSpec((8, D), lambda l, j: (0, 0)),
            pl.BlockSpec((None, D, tn), lambda l, j: (l, 0, j)),
            pl.BlockSpec((None, 1, tn), lambda l, j: (l, 0, j)),
        ],
        out_specs=pl.BlockSpec((None, 8, tn), lambda l, j: (l, 0, j)),
        compiler_params=_cp(("arbitrary", "arbitrary"), 52),
        name="adaln_mods",
    )(cvec, w_ada, b_ada.reshape(DEPTH, 1, 6 * D))


def _mod_spec(layer, k, row_of, width=D, col_of=None):
    if col_of is None:
        return pl.BlockSpec((None, None, None, 1, width),
                            lambda *g: (layer, row_of(*g), k, 0, 0))
    return pl.BlockSpec((None, None, None, 1, width),
                        lambda *g: (layer, row_of(*g), k, 0, col_of(*g)))


def _row_of(ctx, tiles_per_batch, axis):
    if ctx:
        return lambda *g: BATCH
    return lambda *g: g[axis] // tiles_per_batch


def _normmod_kernel(x_ref, g_ref, sh_ref, sc_ref, o_ref):
    y = _rms(x_ref[...], g_ref[...])
    o_ref[...] = (y * (1.0 + sc_ref[...]) + sh_ref[...]).astype(o_ref.dtype)


def _norm_kernel(x_ref, g_ref, o_ref):
    o_ref[...] = _rms(x_ref[...], g_ref[...]).astype(o_ref.dtype)


def _normmod(x, g, mods5, layer, ctx):
    t = x.shape[0]
    tm = 1024
    row_of = _row_of(ctx, SEQ // tm, 0)
    return pl.pallas_call(
        _normmod_kernel,
        out_shape=jax.ShapeDtypeStruct((t, D), BF16),
        grid=(t // tm,),
        in_specs=[
            pl.BlockSpec((tm, D), lambda i: (i, 0)),
            pl.BlockSpec((1, D), lambda i: (0, 0)),
            _mod_spec(layer, 0, row_of),
            _mod_spec(layer, 1, row_of),
        ],
        out_specs=pl.BlockSpec((tm, D), lambda i: (i, 0)),
        compiler_params=_cp(("arbitrary",), 44),
        name="norm_mod",
    )(x, g.reshape(1, D), mods5, mods5)


def _final_norm(x, g):
    t = x.shape[0]
    tm = 1024
    return pl.pallas_call(
        _norm_kernel,
        out_shape=jax.ShapeDtypeStruct((t, D), F32),
        grid=(t // tm,),
        in_specs=[
            pl.BlockSpec((tm, D), lambda i: (i, 0)),
            pl.BlockSpec((1, D), lambda i: (0, 0)),
        ],
        out_specs=pl.BlockSpec((tm, D), lambda i: (i, 0)),
        compiler_params=_cp(("arbitrary",), 48),
        name="final_norm",
    )(x, g.reshape(1, D))


TN_IN = 1024
IN_MCHUNK = {"raw": 1024, "sigmoid": 1024, "gelu": 512}
_ACTS = {"raw": lambda a: a, "gelu": _gelu_tanh, "sigmoid": _sigmoid}


def _inproj_kernel(h_ref, w_ref, o_ref, wb_ref, *, act):
    @pl.when(pl.program_id(1) == 0)
    def _():
        wb_ref[...] = w_ref[...].astype(BF16)

    w = wb_ref[...]
    mc = min(IN_MCHUNK[act], h_ref.shape[0])
    for r0 in range(0, h_ref.shape[0], mc):
        acc = jnp.dot(h_ref[r0:r0 + mc, :], w, preferred_element_type=F32)
        o_ref[r0:r0 + mc, :] = _ACTS[act](acc).astype(o_ref.dtype)


def _inproj(h, w, layer, col_tiles, act):
    t = h.shape[0]
    tm = min(t, 2048)
    first, nj, skip_at, skip = col_tiles

    def wmap(j, i):
        return (layer, 0, first + j + skip * (j // skip_at))

    return pl.pallas_call(
        functools.partial(_inproj_kernel, act=act),
        out_shape=jax.ShapeDtypeStruct((t, nj * TN_IN), BF16),
        grid=(nj, t // tm),
        in_specs=[
            pl.BlockSpec((tm, D), lambda j, i: (i, 0)),
            pl.BlockSpec((None, D, TN_IN), wmap),
        ],
        out_specs=pl.BlockSpec((tm, TN_IN), lambda j, i: (i, j)),
        scratch_shapes=[pltpu.VMEM((D, TN_IN), BF16)],
        compiler_params=_cp(("arbitrary", "arbitrary"), 56),
        name="in_proj_" + act,
    )(h, w)


COLS_RAW = (0, 6, 3, (OFF_B - OFF_U) // TN_IN)
COLS_GELU = (OFF_U // TN_IN, (OFF_B - OFF_U) // TN_IN, 1 << 20, 0)
COLS_GATE = (OFF_G // TN_IN, 3 * D // TN_IN, 1 << 20, 0)
COLS_KV = (OFF_K // TN_IN, 1, 1 << 20, 0)


def _rope_tables():
    rows = SEQ // GRID_W
    row = jnp.repeat(jnp.arange(rows, dtype=F32), GRID_W)
    col = jnp.tile(jnp.arange(GRID_W, dtype=F32), rows)
    n_freq = HD // 4
    inv_freq = ROPE_THETA ** (-jnp.arange(n_freq, dtype=F32) / n_freq)
    ar = row[:, None] * inv_freq
    ac = col[:, None] * inv_freq
    cos = jnp.concatenate([jnp.cos(ar), jnp.cos(ar), jnp.cos(ac), jnp.cos(ac)], axis=-1)
    sin = jnp.concatenate([-jnp.sin(ar), jnp.sin(ar), -jnp.sin(ac), jnp.sin(ac)], axis=-1)
    return cos, sin


def _headnorm_kernel(x_ref, g_ref, *rest, n_heads, scale, rope):
    def first_half_mask(rows):
        lane = jax.lax.broadcasted_iota(jnp.int32, (rows, HD), 1)
        return (lane % (HD // 2)) < (HD // 4)

    def swap_halves(v, mask):
        return jnp.where(mask, pltpu.roll(v, HD - HD // 4, axis=1), pltpu.roll(v, HD // 4, axis=1))

    g = g_ref[...] * scale
    if rope:
        cos_ref, sin_ref, o_ref = rest
        g_partner = swap_halves(jnp.broadcast_to(g, (8, HD)), first_half_mask(8))[0:1, :]
        gcos = cos_ref[...] * g
        gsin = sin_ref[...] * g_partner
        packed_mask = first_half_mask(x_ref.shape[0] // 2)
    else:
        (o_ref,) = rest
    for h in range(n_heads):
        xb = x_ref[:, h * HD:(h + 1) * HD]
        x = xb.astype(F32)
        if rope:
            rstd = jax.lax.rsqrt(jnp.mean(x * x, axis=-1, keepdims=True) + EPS)
            packed = pltpu.bitcast(xb, jnp.uint32)
            partner = pltpu.bitcast(swap_halves(packed, packed_mask), BF16).astype(F32)
            y = (x * gcos + partner * gsin) * rstd
        else:
            y = _rms(x, g)
        o_ref[:, h * HD:(h + 1) * HD] = y.astype(o_ref.dtype)


def _headnorm(src, col_blk, n_heads, g, scale, rope_tabs):
    t = src.shape[0]
    tm = 1024
    w = n_heads * HD
    rope = rope_tabs is not None
    in_specs = [
        pl.BlockSpec((tm, w), lambda i: (i, col_blk)),
        pl.BlockSpec((1, HD), lambda i: (0, 0)),
    ]
    args = [src, g.reshape(1, HD)]
    if rope:
        nt = SEQ // tm
        in_specs += [pl.BlockSpec((tm, HD), lambda i: (i % nt, 0))] * 2
        args += list(rope_tabs)
    return pl.pallas_call(
        functools.partial(_headnorm_kernel, n_heads=n_heads, scale=scale, rope=rope),
        out_shape=jax.ShapeDtypeStruct((t, w), BF16),
        grid=(t // tm,),
        in_specs=in_specs,
        out_specs=pl.BlockSpec((tm, w), lambda i: (i, 0)),
        compiler_params=_cp(("arbitrary",), 44),
        name="head_norm",
    )(*args)


TQ_SUB = 256


KEY_TILE = 256


def _attn_kernel(q_ref, *refs, with_latent):
    if with_latent:
        kx_ref, vx_ref, kc_ref, vc_ref, o_ref, k_s, vt_s = refs
        k_s[0:SEQ, :] = kx_ref[...]
        k_s[SEQ:SEQ + CTX, :] = kc_ref[...]
        vt_s[:, 0:SEQ] = vx_ref[...].T
        vt_s[:, SEQ:SEQ + CTX] = vc_ref[...].T
    else:
        kc_ref, vc_ref, o_ref, k_s, vt_s = refs
        k_s[...] = kc_ref[...]
        vt_s[...] = vc_ref[...].T
    tiles = list(range(0, k_s.shape[0], KEY_TILE))
    units = [(r0, g0) for r0 in range(0, q_ref.shape[0], TQ_SUB) for g0 in range(0, GROUP, 2)]

    def queries(u):
        r0, g0 = units[u]
        return jnp.concatenate([q_ref[r0:r0 + TQ_SUB, g0 * HD:(g0 + 1) * HD],
                                q_ref[r0:r0 + TQ_SUB, (g0 + 1) * HD:(g0 + 2) * HD]], axis=0)

    def scores(q, t0):
        return jax.lax.dot_general(k_s[t0:t0 + KEY_TILE, :], q, (((1,), (1,)), ((), ())),
                                   preferred_element_type=F32)

    def colmax(sts):
        m = None
        for st in sts:
            mc = jnp.max(st, axis=0, keepdims=True)
            m = mc if m is None else jnp.maximum(m, mc)
        return m

    sts = [scores(queries(0), t0) for t0 in tiles]
    m = colmax(sts)
    for u, (r0, g0) in enumerate(units):
        q_next = queries(u + 1) if u + 1 < len(units) else None
        sts_next = []
        l = None
        ot = None
        for t0, st in zip(tiles, sts):
            p = jnp.exp2(st - m)
            lc = jnp.sum(p, axis=0, keepdims=True)
            oc = jnp.dot(vt_s[:, t0:t0 + KEY_TILE], p.astype(BF16),
                         preferred_element_type=F32)
            l = lc if l is None else l + lc
            ot = oc if ot is None else ot + oc
            if q_next is not None:
                sts_next.append(scores(q_next, t0))
        ot = ot * (1.0 / l)
        o_ref[r0:r0 + TQ_SUB, g0 * HD:(g0 + 1) * HD] = ot[:, :TQ_SUB].T.astype(o_ref.dtype)
        o_ref[r0:r0 + TQ_SUB, (g0 + 1) * HD:(g0 + 2) * HD] = ot[:, TQ_SUB:].T.astype(o_ref.dtype)
        if q_next is not None:
            sts = sts_next
            m = colmax(sts)


def _attention(q, lq, latent, context):
    with_latent = latent is not None
    s_len = (SEQ if with_latent else 0) + CTX
    tq = min(lq, 2048)
    nq = lq // tq
    in_specs = [pl.BlockSpec((tq, GROUP * HD), lambda b, h, i: (b * nq + i, h))]
    args = [q]
    for src, rows in ((latent, SEQ), (context, CTX)):
        if src is None:
            continue
        k_arr, v_arr, v_blk = src
        in_specs += [pl.BlockSpec((rows, HD), lambda b, h, i: (b, h)),
                     pl.BlockSpec((rows, HD), lambda b, h, i, v_blk=v_blk: (b, v_blk + h))]
        args += [k_arr, v_arr]
    return pl.pallas_call(
        functools.partial(_attn_kernel, with_latent=with_latent),
        out_shape=jax.ShapeDtypeStruct((BATCH * lq, D), BF16),
        grid=(BATCH, N_KV, nq),
        in_specs=in_specs,
        out_specs=pl.BlockSpec((tq, GROUP * HD), lambda b, h, i: (b * nq + i, h)),
        scratch_shapes=[pltpu.VMEM((s_len, HD), BF16), pltpu.VMEM((HD, s_len), BF16)],
        compiler_params=_cp(("arbitrary", "arbitrary", "arbitrary"), 48),
        name="attention",
    )(*args)


def _gmlp_kernel(u_ref, v_ref, gv_ref, ws_ref, bs_ref, o_ref, *, tm):
    v = _rms(v_ref[...].astype(F32), gv_ref[...]).astype(BF16)
    for g in range(GG):
        c0 = g * CHUNK
        w = ws_ref[g].astype(BF16)
        for n in range(tm // CHUNK):
            r0 = n * CHUNK
            mixed = jnp.dot(w, v[r0:r0 + CHUNK, c0:c0 + CHUNK],
                            preferred_element_type=F32) + bs_ref[g]
            u = u_ref[r0:r0 + CHUNK, c0:c0 + CHUNK].astype(F32)
            o_ref[r0:r0 + CHUNK, c0:c0 + CHUNK] = (u * mixed).astype(o_ref.dtype)


def _gmlp(zg, gv, ws, bs):
    t = zg.shape[0]
    tm = 512
    return pl.pallas_call(
        functools.partial(_gmlp_kernel, tm=tm),
        out_shape=jax.ShapeDtypeStruct((t, GW), BF16),
        grid=(t // tm,),
        in_specs=[
            pl.BlockSpec((tm, GW), lambda i: (i, 0)),
            pl.BlockSpec((tm, GW), lambda i: (i, 1)),
            pl.BlockSpec((1, GW), lambda i: (0, 0)),
            pl.BlockSpec((GG, CHUNK, CHUNK), lambda i: (0, 0, 0)),
            pl.BlockSpec((GG, CHUNK, CHUNK), lambda i: (0, 0, 0)),
        ],
        out_specs=pl.BlockSpec((tm, GW), lambda i: (i, 0)),
        compiler_params=_cp(("arbitrary",), 32),
        name="gmlp",
    )(zg, zg, gv.reshape(1, GW), ws, bs)


def _dwconv3(x, w, seq_len):
    rows = x.shape[0]
    pos = jax.lax.broadcasted_iota(jnp.int32, x.shape, 0) % seq_len
    prev = jnp.where(pos == 0, 0.0, pltpu.roll(x, 1, axis=0))
    nxt = jnp.where(pos == seq_len - 1, 0.0, pltpu.roll(x, rows - 1, axis=0))
    return prev * w[0:1, :] + x * w[1:2, :] + nxt * w[2:3, :]


def _sconv_kernel(b_ref, c_ref, h_ref, w_ref, o_ref, *, seq_len):
    t = c_ref[...].astype(F32) * h_ref[...].astype(F32)
    y = b_ref[...].astype(F32) * _dwconv3(t, w_ref[...], seq_len)
    o_ref[...] = y.astype(o_ref.dtype)


def _sconv(raw, w, seq_len, rows):
    t = raw.shape[0]
    tc = 256
    return pl.pallas_call(
        functools.partial(_sconv_kernel, seq_len=seq_len),
        out_shape=jax.ShapeDtypeStruct((t, CW), BF16),
        grid=(t // rows, CW // tc),
        in_specs=[
            pl.BlockSpec((rows, tc), lambda i, j: (i, R_B // tc + j)),
            pl.BlockSpec((rows, tc), lambda i, j: (i, R_C // tc + j)),
            pl.BlockSpec((rows, tc), lambda i, j: (i, R_H // tc + j)),
            pl.BlockSpec((3, tc), lambda i, j: (0, j)),
        ],
        out_specs=pl.BlockSpec((rows, tc), lambda i, j: (i, j)),
        compiler_params=_cp(("arbitrary", "arbitrary"), 40),
        name="short_conv",
    )(raw, raw, raw, w)


def _merge_kernel(ya_ref, yb_ref, at_ref, wa_ref, wb_ref, wc_ref,
                  ga_ref, gb_ref, gc_ref, o_ref, wa_s, wb_s, wc_s):
    @pl.when(pl.program_id(1) == 0)
    def _():
        wa_s[...] = wa_ref[...].astype(BF16)
        wb_s[...] = wb_ref[...].astype(BF16)
        wc_s[...] = wc_ref[...].astype(BF16)

    a = jnp.dot(ya_ref[...], wa_s[...], preferred_element_type=F32)
    m = ga_ref[...].astype(F32) * a
    b = jnp.dot(yb_ref[...], wb_s[...], preferred_element_type=F32)
    m = m + gb_ref[...].astype(F32) * b
    c = jnp.dot(at_ref[...], wc_s[...], preferred_element_type=F32)
    m = m + gc_ref[...].astype(F32) * c
    o_ref[...] = m.astype(o_ref.dtype)


def _merge(ya, yb, attn, gates, w_pa, w_pb, w_pc, layer):
    t = ya.shape[0]
    tm = 1024
    tn = 512
    nb = D // tn
    return pl.pallas_call(
        _merge_kernel,
        out_shape=jax.ShapeDtypeStruct((t, D), BF16),
        grid=(nb, t // tm),
        in_specs=[
            pl.BlockSpec((tm, GW), lambda j, i: (i, 0)),
            pl.BlockSpec((tm, CW), lambda j, i: (i, 0)),
            pl.BlockSpec((tm, D), lambda j, i: (i, 0)),
            pl.BlockSpec((None, GW, tn), lambda j, i: (layer, 0, j)),
            pl.BlockSpec((None, CW, tn), lambda j, i: (layer, 0, j)),
            pl.BlockSpec((None, D, tn), lambda j, i: (layer, 0, j)),
            pl.BlockSpec((tm, tn), lambda j, i: (i, j)),
            pl.BlockSpec((tm, tn), lambda j, i: (i, nb + j)),
            pl.BlockSpec((tm, tn), lambda j, i: (i, 2 * nb + j)),
        ],
        out_specs=pl.BlockSpec((tm, tn), lambda j, i: (i, j)),
        scratch_shapes=[pltpu.VMEM((GW, tn), BF16), pltpu.VMEM((CW, tn), BF16),
                        pltpu.VMEM((D, tn), BF16)],
        compiler_params=_cp(("arbitrary", "arbitrary"), 56),
        name="merge",
    )(ya, yb, attn, w_pa, w_pb, w_pc, gates, gates, gates)


def _outproj_kernel(m_ref, w_ref, x_ref, gt_ref, g_ref, sh_ref, sc_ref, x1_ref, h_ref, w_s):
    @pl.when(pl.program_id(0) == 0)
    def _():
        w_s[...] = w_ref[...].astype(BF16)

    y = jnp.dot(m_ref[...], w_s[...], preferred_element_type=F32)
    x1 = x_ref[...] + gt_ref[...] * y
    x1_ref[...] = x1
    h_ref[...] = (_rms(x1, g_ref[...]) * (1.0 + sc_ref[...]) + sh_ref[...]).astype(h_ref.dtype)


def _outproj(m, w_o, x, g_ffn, mods5, layer, ctx):
    t = m.shape[0]
    tm = 512
    row_of = _row_of(ctx, SEQ // tm, 0)
    return pl.pallas_call(
        _outproj_kernel,
        out_shape=(jax.ShapeDtypeStruct((t, D), F32), jax.ShapeDtypeStruct((t, D), BF16)),
        grid=(t // tm,),
        in_specs=[
            pl.BlockSpec((tm, D), lambda i: (i, 0)),
            pl.BlockSpec((None, D, D), lambda i: (layer, 0, 0), pipeline_mode=pl.Buffered(1)),
            pl.BlockSpec((tm, D), lambda i: (i, 0)),
            _mod_spec(layer, 2, row_of),
            pl.BlockSpec((1, D), lambda i: (0, 0)),
            _mod_spec(layer, 3, row_of),
            _mod_spec(layer, 4, row_of),
        ],
        out_specs=(pl.BlockSpec((tm, D), lambda i: (i, 0)),
                   pl.BlockSpec((tm, D), lambda i: (i, 0))),
        scratch_shapes=[pltpu.VMEM((D, D), BF16)],
        compiler_params=_cp(("arbitrary",), 56),
        name="out_proj",
    )(m, w_o, x, mods5, g_ffn.reshape(1, D), mods5, mods5)


FF_BLOCKS = D_FF // LANES
FF_SUBW = 2 * LANES
FF_NSUB = 2
TN_FF = FF_NSUB * FF_SUBW
FF_NBLK = TN_FF // LANES
FF_MCHUNK = 256


def _chunked_dot(h_ref, w):
    parts = [jnp.dot(h_ref[r0:r0 + FF_MCHUNK, :], w, preferred_element_type=F32)
             for r0 in range(0, h_ref.shape[0], FF_MCHUNK)]
    return jnp.concatenate(parts, axis=0)


def _ffn_up_kernel(h_ref, *refs, seq_len):
    a_refs = refs[0:FF_NBLK]
    b_refs = refs[FF_NBLK:2 * FF_NBLK]
    ca_refs = refs[2 * FF_NBLK:3 * FF_NBLK]
    cb_refs = refs[3 * FF_NBLK:4 * FF_NBLK]
    o_ref = refs[4 * FF_NBLK]
    ws = []
    for s in range(FF_NSUB):
        k0 = s * (FF_SUBW // LANES)
        k1 = k0 + FF_SUBW // LANES
        ws.append((jnp.concatenate([r[...] for r in a_refs[k0:k1]], axis=1).astype(BF16),
                   jnp.concatenate([r[...] for r in b_refs[k0:k1]], axis=1).astype(BF16),
                   jnp.concatenate([r[...] for r in ca_refs[k0:k1]], axis=1),
                   jnp.concatenate([r[...] for r in cb_refs[k0:k1]], axis=1)))
    for s in range(FF_NSUB):
        wa, wb, ca, cb = ws[s]
        a = _dwconv3(_chunked_dot(h_ref, wa), ca, seq_len)
        b = _dwconv3(_chunked_dot(h_ref, wb), cb, seq_len)
        o_ref[:, s * FF_SUBW:(s + 1) * FF_SUBW] = (a * _sigmoid(a) * b).astype(o_ref.dtype)


def _ffn_up(h, w_up, w_conv, layer, seq_len, rows):
    t = h.shape[0]
    nj = pl.cdiv(D_FF, TN_FF)
    last = 2 * FF_BLOCKS - 1

    def wspec(rows_, off, k):
        return pl.BlockSpec((None, rows_, LANES),
                            lambda i, j: (layer, 0, jnp.minimum(off + FF_NBLK * j + k, last)))

    blocks = range(FF_NBLK)
    specs = ([wspec(D, 0, k) for k in blocks] + [wspec(D, FF_BLOCKS, k) for k in blocks]
             + [wspec(3, 0, k) for k in blocks] + [wspec(3, FF_BLOCKS, k) for k in blocks])
    return pl.pallas_call(
        functools.partial(_ffn_up_kernel, seq_len=seq_len),
        out_shape=jax.ShapeDtypeStruct((t, D_FF), BF16),
        grid=(t // rows, nj),
        in_specs=[pl.BlockSpec((rows, D), lambda i, j: (i, 0))] + specs,
        out_specs=pl.BlockSpec((rows, TN_FF), lambda i, j: (i, j)),
        compiler_params=_cp(("arbitrary", "arbitrary"), 56),
        name="ffn_up",
    )(h, *([w_up] * (2 * FF_NBLK)), *([w_conv] * (2 * FF_NBLK)))


def _ffn_down_kernel(g_ref, w_ref, x_ref, gt_ref, o_ref, w_s):
    @pl.when(pl.program_id(1) == 0)
    def _():
        w_s[...] = w_ref[...].astype(BF16)

    y = jnp.dot(g_ref[...], w_s[...], preferred_element_type=F32)
    o_ref[...] = x_ref[...] + gt_ref[...] * y


def _ffn_down(g, w_down, x, mods5, layer, ctx):
    t = g.shape[0]
    tm = 512
    tn = 512
    row_of = _row_of(ctx, SEQ // tm, 1)
    return pl.pallas_call(
        _ffn_down_kernel,
        out_shape=jax.ShapeDtypeStruct((t, D), F32),
        grid=(D // tn, t // tm),
        in_specs=[
            pl.BlockSpec((tm, D_FF), lambda j, i: (i, 0)),
            pl.BlockSpec((None, D_FF, tn), lambda j, i: (layer, 0, j)),
            pl.BlockSpec((tm, tn), lambda j, i: (i, j)),
            _mod_spec(layer, 5, row_of, tn, lambda j, i: j),
        ],
        out_specs=pl.BlockSpec((tm, tn), lambda j, i: (i, j)),
        scratch_shapes=[pltpu.VMEM((D_FF, tn), BF16)],
        compiler_params=_cp(("arbitrary", "arbitrary"), 54),
        name="ffn_down",
    )(g, w_down, x, mods5)


def kernel(x, c, ctx, c_ctx, w_ada, b_ada, g_mix, w_in, g_q, g_k, w_gmlp, b_gmlp, g_gmlp_v,
           w_sconv, w_pa, w_pb, w_pc, w_o, g_ffn, w_up, w_ffn_conv, w_down, g_final):
    cvec = jnp.concatenate([c, c_ctx[None, :], jnp.zeros((3, D), F32)], axis=0)
    mods5 = _mods(cvec, w_ada, b_ada).reshape(DEPTH, 8, 6, 1, D)
    rope_tabs = _rope_tables()
    q_scale = HD ** -0.5 * math.log2(math.e)

    xs = x.reshape(BATCH * SEQ, D)
    cs = ctx.reshape(BATCH * CTX, D)

    for l in range(DEPTH):
        last = l == DEPTH - 1
        bs = jnp.broadcast_to(b_gmlp[l][:, :, None], (GG, CHUNK, CHUNK))
        hx = _normmod(xs, g_mix[l], mods5, l, False)
        hc = _normmod(cs, g_mix[l], mods5, l, True)

        raw_x = _inproj(hx, w_in, l, COLS_RAW, "raw")
        zg_x = _inproj(hx, w_in, l, COLS_GELU, "gelu")
        gates_x = _inproj(hx, w_in, l, COLS_GATE, "sigmoid")
        if not last:
            raw_c = _inproj(hc, w_in, l, COLS_RAW, "raw")
            kc = _headnorm(raw_c, R_K // KV_DIM, N_KV, g_k[l], 1.0, None)
            ctx_kv = (kc, raw_c, R_V // HD)
        else:
            kv_c = _inproj(hc, w_in, l, COLS_KV, "raw")
            kc = _headnorm(kv_c, 0, N_KV, g_k[l], 1.0, None)
            ctx_kv = (kc, kv_c, KV_DIM // HD)
        qx = _headnorm(raw_x, 0, N_HEADS, g_q[l], q_scale, rope_tabs)
        kx = _headnorm(raw_x, R_K // KV_DIM, N_KV, g_k[l], 1.0, rope_tabs)
        attn_x = _attention(qx, SEQ, (kx, raw_x, R_V // HD), ctx_kv)
        ya = _gmlp(zg_x, g_gmlp_v[l], w_gmlp[l], bs)
        yb = _sconv(raw_x, w_sconv[l], SEQ, SEQ)
        m = _merge(ya, yb, attn_x, gates_x, w_pa, w_pb, w_pc, l)
        x1, fx = _outproj(m, w_o, xs, g_ffn[l], mods5, l, False)

        gx = _ffn_up(fx, w_up, w_ffn_conv, l, SEQ, SEQ)
        xs = _ffn_down(gx, w_down, x1, mods5, l, False)

        if not last:
            zg_c = _inproj(hc, w_in, l, COLS_GELU, "gelu")
            gates_c = _inproj(hc, w_in, l, COLS_GATE, "sigmoid")
            qc = _headnorm(raw_c, 0, N_HEADS, g_q[l], q_scale, None)
            attn_c = _attention(qc, CTX, None, ctx_kv)
            yac = _gmlp(zg_c, g_gmlp_v[l], w_gmlp[l], bs)
            ybc = _sconv(raw_c, w_sconv[l], CTX, BATCH * CTX)
            mc = _merge(yac, ybc, attn_c, gates_c, w_pa, w_pb, w_pc, l)
            c1, fc = _outproj(mc, w_o, cs, g_ffn[l], mods5, l, True)
            gc = _ffn_up(fc, w_up, w_ffn_conv, l, CTX, BATCH * CTX)
            cs = _ffn_down(gc, w_down, c1, mods5, l, True)

    return _final_norm(xs, g_final).reshape(BATCH, SEQ, D)
```

```python
import functools
import math

import jax
import jax.numpy as jnp
from jax.experimental import pallas as pl
from jax.experimental.pallas import tpu as pltpu

F32 = jnp.float32
BF16 = jnp.bfloat16

D = 2048
BATCH = 4
SEQ = 2048
CTX = 256
DEPTH = 2
GRID_W = 64
N_HEADS = 16
N_KV = 4
HD = 128
GROUP = N_HEADS // N_KV
CHUNK = 128
GW = D // 2
GG = 8
CW = D // 2
D_FF = 5504
EPS = 1e-6
ROPE_THETA = 10000.0
KV_DIM = N_KV * HD
IN_DIM = D + 2 * KV_DIM + 2 * GW + 3 * CW + 3 * D
OFF_K = D
OFF_U = D + 2 * KV_DIM
OFF_B = OFF_U + 2 * GW
OFF_G = OFF_B + 3 * CW
R_K = D
R_V = D + KV_DIM
R_B = D + 2 * KV_DIM
R_C = R_B + CW
R_H = R_C + CW
LANES = 128

VMEM_PHYS_V7X = 64 * 1024 * 1024
VMEM_CAP = VMEM_PHYS_V7X - 8 * 1024 * 1024


def _cp(sem, vmem_mb):
    return pltpu.CompilerParams(
        dimension_semantics=sem,
        vmem_limit_bytes=min(int(vmem_mb * 1024 * 1024), VMEM_CAP))


def _rms(xf, g):
    ms = jnp.mean(xf * xf, axis=-1, keepdims=True)
    return xf * jax.lax.rsqrt(ms + EPS) * g


def _sigmoid(x):
    return 0.5 * jnp.tanh(0.5 * x) + 0.5


def _gelu_tanh(x):
    c = math.sqrt(2.0 / math.pi)
    return 0.5 * x * (1.0 + jnp.tanh(c * (x + 0.044715 * (x * x * x))))


def _mods_kernel(c_ref, w_ref, b_ref, o_ref):
    c = c_ref[...]
    s = (c * _sigmoid(c)).astype(BF16)
    w = w_ref[...].astype(BF16)
    o_ref[...] = jnp.dot(s, w, preferred_element_type=F32) + b_ref[...]


def _mods(cvec, w_ada, b_ada):
    tn = 2048
    nj = 6 * D // tn
    return pl.pallas_call(
        _mods_kernel,
        out_shape=jax.ShapeDtypeStruct((DEPTH, 8, 6 * D), F32),
        grid=(DEPTH, nj),
        in_specs=[
            pl.BlockSpec((8, D), lambda l, j: (0, 0)),
            pl.BlockSpec((None, D, tn), lambda l, j: (l, 0, j)),
            pl.BlockSpec((None, 1, tn), lambda l, j: (l, 0, j)),
        ],
        out_specs=pl.BlockSpec((None, 8, tn), lambda l, j: (l, 0, j)),
        compiler_params=_cp(("arbitrary", "arbitrary"), 52),
        name="adaln_mods",
    )(cvec, w_ada, b_ada.reshape(DEPTH, 1, 6 * D))


def _mod_spec(layer, k, row_of, width=D, col_of=None):
    if col_of is None:
        return pl.BlockSpec((None, None, None, 1, width),
                            lambda *g: (layer, row_of(*g), k, 0, 0))
    return pl.BlockSpec((None, None, None, 1, width),
                        lambda *g: (layer, row_of(*g), k, 0, col_of(*g)))


def _row_of(ctx, tiles_per_batch, axis):
    if ctx:
        return lambda *g: BATCH
    return lambda *g: g[axis] // tiles_per_batch


def _normmod_kernel(x_ref, g_ref, sh_ref, sc_ref, o_ref):
    y = _rms(x_ref[...], g_ref[...])
    o_ref[...] = (y * (1.0 + sc_ref[...]) + sh_ref[...]).astype(o_ref.dtype)


def _norm_kernel(x_ref, g_ref, o_ref):
    o_ref[...] = _rms(x_ref[...], g_ref[...]).astype(o_ref.dtype)


def _normmod(x, g, mods5, layer, ctx):
    t = x.shape[0]
    tm = 1024
    row_of = _row_of(ctx, SEQ // tm, 0)
    return pl.pallas_call(
        _normmod_kernel,
        out_shape=jax.ShapeDtypeStruct((t, D), BF16),
        grid=(t // tm,),
        in_specs=[
            pl.BlockSpec((tm, D), lambda i: (i, 0)),
            pl.BlockSpec((1, D), lambda i: (0, 0)),
            _mod_spec(layer, 0, row_of),
            _mod_spec(layer, 1, row_of),
        ],
        out_specs=pl.BlockSpec((tm, D), lambda i: (i, 0)),
        compiler_params=_cp(("arbitrary",), 44),
        name="norm_mod",
    )(x, g.reshape(1, D), mods5, mods5)


def _final_norm(x, g):
    t = x.shape[0]
    tm = 1024
    return pl.pallas_call(
        _norm_kernel,
        out_shape=jax.ShapeDtypeStruct((t, D), F32),
        grid=(t // tm,),
        in_specs=[
            pl.BlockSpec((tm, D), lambda i: (i, 0)),
            pl.BlockSpec((1, D), lambda i: (0, 0)),
        ],
        out_specs=pl.BlockSpec((tm, D), lambda i: (i, 0)),
        compiler_params=_cp(("arbitrary",), 48),
        name="final_norm",
    )(x, g.reshape(1, D))


TN_IN = 1024
IN_MCHUNK = {"raw": 1024, "sigmoid": 1024, "gelu": 512}
_ACTS = {"raw": lambda a: a, "gelu": _gelu_tanh, "sigmoid": _sigmoid}


def _inproj_kernel(h_ref, w_ref, o_ref, wb_ref, *, act):
    @pl.when(pl.program_id(1) == 0)
    def _():
        wb_ref[...] = w_ref[...].astype(BF16)

    w = wb_ref[...]
    mc = min(IN_MCHUNK[act], h_ref.shape[0])
    for r0 in range(0, h_ref.shape[0], mc):
        acc = jnp.dot(h_ref[r0:r0 + mc, :], w, preferred_element_type=F32)
        o_ref[r0:r0 + mc, :] = _ACTS[act](acc).astype(o_ref.dtype)


def _inproj(h, w, layer, col_tiles, act):
    t = h.shape[0]
    tm = min(t, 2048)
    first, nj, skip_at, skip = col_tiles

    def wmap(j, i):
        return (layer, 0, first + j + skip * (j // skip_at))

    return pl.pallas_call(
        functools.partial(_inproj_kernel, act=act),
        out_shape=jax.ShapeDtypeStruct((t, nj * TN_IN), BF16),
        grid=(nj, t // tm),
        in_specs=[
            pl.BlockSpec((tm, D), lambda j, i: (i, 0)),
            pl.BlockSpec((None, D, TN_IN), wmap),
        ],
        out_specs=pl.BlockSpec((tm, TN_IN), lambda j, i: (i, j)),
        scratch_shapes=[pltpu.VMEM((D, TN_IN), BF16)],
        compiler_params=_cp(("arbitrary", "arbitrary"), 56),
        name="in_proj_" + act,
    )(h, w)


COLS_RAW = (0, 6, 3, (OFF_B - OFF_U) // TN_IN)
COLS_GELU = (OFF_U // TN_IN, (OFF_B - OFF_U) // TN_IN, 1 << 20, 0)
COLS_GATE = (OFF_G // TN_IN, 3 * D // TN_IN, 1 << 20, 0)
COLS_KV = (OFF_K // TN_IN, 1, 1 << 20, 0)


def _rope_tables():
    rows = SEQ // GRID_W
    row = jnp.repeat(jnp.arange(rows, dtype=F32), GRID_W)
    col = jnp.tile(jnp.arange(GRID_W, dtype=F32), rows)
    n_freq = HD // 4
    inv_freq = ROPE_THETA ** (-jnp.arange(n_freq, dtype=F32) / n_freq)
    ar = row[:, None] * inv_freq
    ac = col[:, None] * inv_freq
    cos = jnp.concatenate([jnp.cos(ar), jnp.cos(ar), jnp.cos(ac), jnp.cos(ac)], axis=-1)
    sin = jnp.concatenate([-jnp.sin(ar), jnp.sin(ar), -jnp.sin(ac), jnp.sin(ac)], axis=-1)
    return cos, sin


def _headnorm_kernel(x_ref, g_ref, *rest, n_heads, scale, rope):
    def first_half_mask(rows):
        lane = jax.lax.broadcasted_iota(jnp.int32, (rows, HD), 1)
        return (lane % (HD // 2)) < (HD // 4)

    def swap_halves(v, mask):
        return jnp.where(mask, pltpu.roll(v, HD - HD // 4, axis=1), pltpu.roll(v, HD // 4, axis=1))

    g = g_ref[...] * scale
    if rope:
        cos_ref, sin_ref, o_ref = rest
        g_partner = swap_halves(jnp.broadcast_to(g, (8, HD)), first_half_mask(8))[0:1, :]
        gcos = cos_ref[...] * g
        gsin = sin_ref[...] * g_partner
        packed_mask = first_half_mask(x_ref.shape[0] // 2)
    else:
        (o_ref,) = rest
    for h in range(n_heads):
        xb = x_ref[:, h * HD:(h + 1) * HD]
        x = xb.astype(F32)
        if rope:
            rstd = jax.lax.rsqrt(jnp.mean(x * x, axis=-1, keepdims=True) + EPS)
            packed = pltpu.bitcast(xb, jnp.uint32)
            partner = pltpu.bitcast(swap_halves(packed, packed_mask), BF16).astype(F32)
            y = (x * gcos + partner * gsin) * rstd
        else:
            y = _rms(x, g)
        o_ref[:, h * HD:(h + 1) * HD] = y.astype(o_ref.dtype)


def _headnorm(src, col_blk, n_heads, g, scale, rope_tabs):
    t = src.shape[0]
    tm = 1024
    w = n_heads * HD
    rope = rope_tabs is not None
    in_specs = [
        pl.BlockSpec((tm, w), lambda i: (i, col_blk)),
        pl.BlockSpec((1, HD), lambda i: (0, 0)),
    ]
    args = [src, g.reshape(1, HD)]
    if rope:
        nt = SEQ // tm
        in_specs += [pl.BlockSpec((tm, HD), lambda i: (i % nt, 0))] * 2
        args += list(rope_tabs)
    return pl.pallas_call(
        functools.partial(_headnorm_kernel, n_heads=n_heads, scale=scale, rope=rope),
        out_shape=jax.ShapeDtypeStruct((t, w), BF16),
        grid=(t // tm,),
        in_specs=in_specs,
        out_specs=pl.BlockSpec((tm, w), lambda i: (i, 0)),
        compiler_params=_cp(("arbitrary",), 44),
        name="head_norm",
    )(*args)


TQ_SUB = 256


KEY_TILE = 256


def _attn_kernel(q_ref, *refs, with_latent):
    if with_latent:
        kx_ref, vx_ref, kc_ref, vc_ref, o_ref, k_s, vt_s = refs
        k_s[0:SEQ, :] = kx_ref[...]
        k_s[SEQ:SEQ + CTX, :] = kc_ref[...]
        vt_s[:, 0:SEQ] = vx_ref[...].T
        vt_s[:, SEQ:SEQ + CTX] = vc_ref[...].T
    else:
        kc_ref, vc_ref, o_ref, k_s, vt_s = refs
        k_s[...] = kc_ref[...]
        vt_s[...] = vc_ref[...].T
    tiles = list(range(0, k_s.shape[0], KEY_TILE))
    units = [(r0, g0) for r0 in range(0, q_ref.shape[0], TQ_SUB) for g0 in range(0, GROUP, 2)]

    def queries(u):
        r0, g0 = units[u]
        return jnp.concatenate([q_ref[r0:r0 + TQ_SUB, g0 * HD:(g0 + 1) * HD],
                                q_ref[r0:r0 + TQ_SUB, (g0 + 1) * HD:(g0 + 2) * HD]], axis=0)

    def scores(q, t0):
        return jax.lax.dot_general(k_s[t0:t0 + KEY_TILE, :], q, (((1,), (1,)), ((), ())),
                                   preferred_element_type=F32)

    def colmax(sts):
        m = None
        for st in sts:
            mc = jnp.max(st, axis=0, keepdims=True)
            m = mc if m is None else jnp.maximum(m, mc)
        return m

    sts = [scores(queries(0), t0) for t0 in tiles]
    m = colmax(sts)
    for u, (r0, g0) in enumerate(units):
        q_next = queries(u + 1) if u + 1 < len(units) else None
        sts_next = []
        l = None
        ot = None
        for t0, st in zip(tiles, sts):
            p = jnp.exp2(st - m)
            lc = jnp.sum(p, axis=0, keepdims=True)
            oc = jnp.dot(vt_s[:, t0:t0 + KEY_TILE], p.astype(BF16),
                         preferred_element_type=F32)
            l = lc if l is None else l + lc
            ot = oc if ot is None else ot + oc
            if q_next is not None:
                sts_next.append(scores(q_next, t0))
        ot = ot * (1.0 / l)
        o_ref[r0:r0 + TQ_SUB, g0 * HD:(g0 + 1) * HD] = ot[:, :TQ_SUB].T.astype(o_ref.dtype)
        o_ref[r0:r0 + TQ_SUB, (g0 + 1) * HD:(g0 + 2) * HD] = ot[:, TQ_SUB:].T.astype(o_ref.dtype)
        if q_next is not None:
            sts = sts_next
            m = colmax(sts)


def _attention(q, lq, latent, context):
    with_latent = latent is not None
    s_len = (SEQ if with_latent else 0) + CTX
    tq = min(lq, 2048)
    nq = lq // tq
    in_specs = [pl.BlockSpec((tq, GROUP * HD), lambda b, h, i: (b * nq + i, h))]
    args = [q]
    for src, rows in ((latent, SEQ), (context, CTX)):
        if src is None:
            continue
        k_arr, v_arr, v_blk = src
        in_specs += [pl.BlockSpec((rows, HD), lambda b, h, i: (b, h)),
                     pl.BlockSpec((rows, HD), lambda b, h, i, v_blk=v_blk: (b, v_blk + h))]
        args += [k_arr, v_arr]
    return pl.pallas_call(
        functools.partial(_attn_kernel, with_latent=with_latent),
        out_shape=jax.ShapeDtypeStruct((BATCH * lq, D), BF16),
        grid=(BATCH, N_KV, nq),
        in_specs=in_specs,
        out_specs=pl.BlockSpec((tq, GROUP * HD), lambda b, h, i: (b * nq + i, h)),
        scratch_shapes=[pltpu.VMEM((s_len, HD), BF16), pltpu.VMEM((HD, s_len), BF16)],
        compiler_params=_cp(("arbitrary", "arbitrary", "arbitrary"), 48),
        name="attention",
    )(*args)


def _gmlp_kernel(u_ref, v_ref, gv_ref, ws_ref, bs_ref, o_ref, *, tm):
    v = _rms(v_ref[...].astype(F32), gv_ref[...]).astype(BF16)
    for g in range(GG):
        c0 = g * CHUNK
        w = ws_ref[g].astype(BF16)
        for n in range(tm // CHUNK):
            r0 = n * CHUNK
            mixed = jnp.dot(w, v[r0:r0 + CHUNK, c0:c0 + CHUNK],
                            preferred_element_type=F32) + bs_ref[g]
            u = u_ref[r0:r0 + CHUNK, c0:c0 + CHUNK].astype(F32)
            o_ref[r0:r0 + CHUNK, c0:c0 + CHUNK] = (u * mixed).astype(o_ref.dtype)


def _gmlp(zg, gv, ws, bs):
    t = zg.shape[0]
    tm = 1024
    return pl.pallas_call(
        functools.partial(_gmlp_kernel, tm=tm),
        out_shape=jax.ShapeDtypeStruct((t, GW), BF16),
        grid=(t // tm,),
        in_specs=[
            pl.BlockSpec((tm, GW), lambda i: (i, 0)),
            pl.BlockSpec((tm, GW), lambda i: (i, 1)),
            pl.BlockSpec((1, GW), lambda i: (0, 0)),
            pl.BlockSpec((GG, CHUNK, CHUNK), lambda i: (0, 0, 0)),
            pl.BlockSpec((GG, CHUNK, CHUNK), lambda i: (0, 0, 0)),
        ],
        out_specs=pl.BlockSpec((tm, GW), lambda i: (i, 0)),
        compiler_params=_cp(("arbitrary",), 32),
        name="gmlp",
    )(zg, zg, gv.reshape(1, GW), ws, bs)


def _dwconv3(x, w, seq_len):
    rows = x.shape[0]
    pos = jax.lax.broadcasted_iota(jnp.int32, x.shape, 0) % seq_len
    prev = jnp.where(pos == 0, 0.0, pltpu.roll(x, 1, axis=0))
    nxt = jnp.where(pos == seq_len - 1, 0.0, pltpu.roll(x, rows - 1, axis=0))
    return prev * w[0:1, :] + x * w[1:2, :] + nxt * w[2:3, :]


def _sconv_kernel(b_ref, c_ref, h_ref, w_ref, o_ref, *, seq_len):
    t = c_ref[...].astype(F32) * h_ref[...].astype(F32)
    y = b_ref[...].astype(F32) * _dwconv3(t, w_ref[...], seq_len)
    o_ref[...] = y.astype(o_ref.dtype)


def _sconv(raw, w, seq_len, rows):
    t = raw.shape[0]
    tc = 512
    return pl.pallas_call(
        functools.partial(_sconv_kernel, seq_len=seq_len),
        out_shape=jax.ShapeDtypeStruct((t, CW), BF16),
        grid=(t // rows, CW // tc),
        in_specs=[
            pl.BlockSpec((rows, tc), lambda i, j: (i, R_B // tc + j)),
            pl.BlockSpec((rows, tc), lambda i, j: (i, R_C // tc + j)),
            pl.BlockSpec((rows, tc), lambda i, j: (i, R_H // tc + j)),
            pl.BlockSpec((3, tc), lambda i, j: (0, j)),
        ],
        out_specs=pl.BlockSpec((rows, tc), lambda i, j: (i, j)),
        compiler_params=_cp(("arbitrary", "arbitrary"), 40),
        name="short_conv",
    )(raw, raw, raw, w)


def _merge_kernel(ya_ref, yb_ref, at_ref, wa_ref, wb_ref, wc_ref,
                  ga_ref, gb_ref, gc_ref, o_ref, wa_s, wb_s, wc_s):
    @pl.when(pl.program_id(1) == 0)
    def _():
        wa_s[...] = wa_ref[...].astype(BF16)
        wb_s[...] = wb_ref[...].astype(BF16)
        wc_s[...] = wc_ref[...].astype(BF16)

    a = jnp.dot(ya_ref[...], wa_s[...], preferred_element_type=F32)
    m = ga_ref[...].astype(F32) * a
    b = jnp.dot(yb_ref[...], wb_s[...], preferred_element_type=F32)
    m = m + gb_ref[...].astype(F32) * b
    c = jnp.dot(at_ref[...], wc_s[...], preferred_element_type=F32)
    m = m + gc_ref[...].astype(F32) * c
    o_ref[...] = m.astype(o_ref.dtype)


def _merge(ya, yb, attn, gates, w_pa, w_pb, w_pc, layer):
    t = ya.shape[0]
    tm = 1024
    tn = 512
    nb = D // tn
    return pl.pallas_call(
        _merge_kernel,
        out_shape=jax.ShapeDtypeStruct((t, D), BF16),
        grid=(nb, t // tm),
        in_specs=[
            pl.BlockSpec((tm, GW), lambda j, i: (i, 0)),
            pl.BlockSpec((tm, CW), lambda j, i: (i, 0)),
            pl.BlockSpec((tm, D), lambda j, i: (i, 0)),
            pl.BlockSpec((None, GW, tn), lambda j, i: (layer, 0, j)),
            pl.BlockSpec((None, CW, tn), lambda j, i: (layer, 0, j)),
            pl.BlockSpec((None, D, tn), lambda j, i: (layer, 0, j)),
            pl.BlockSpec((tm, tn), lambda j, i: (i, j)),
            pl.BlockSpec((tm, tn), lambda j, i: (i, nb + j)),
            pl.BlockSpec((tm, tn), lambda j, i: (i, 2 * nb + j)),
        ],
        out_specs=pl.BlockSpec((tm, tn), lambda j, i: (i, j)),
        scratch_shapes=[pltpu.VMEM((GW, tn), BF16), pltpu.VMEM((CW, tn), BF16),
                        pltpu.VMEM((D, tn), BF16)],
        compiler_params=_cp(("arbitrary", "arbitrary"), 56),
        name="merge",
    )(ya, yb, attn, w_pa, w_pb, w_pc, gates, gates, gates)


def _outproj_kernel(m_ref, w_ref, x_ref, gt_ref, g_ref, sh_ref, sc_ref, x1_ref, h_ref, w_s):
    @pl.when(pl.program_id(0) == 0)
    def _():
        w_s[...] = w_ref[...].astype(BF16)

    y = jnp.dot(m_ref[...], w_s[...], preferred_element_type=F32)
    x1 = x_ref[...] + gt_ref[...] * y
    x1_ref[...] = x1
    h_ref[...] = (_rms(x1, g_ref[...]) * (1.0 + sc_ref[...]) + sh_ref[...]).astype(h_ref.dtype)


def _outproj(m, w_o, x, g_ffn, mods5, layer, ctx):
    t = m.shape[0]
    tm = 512
    row_of = _row_of(ctx, SEQ // tm, 0)
    return pl.pallas_call(
        _outproj_kernel,
        out_shape=(jax.ShapeDtypeStruct((t, D), F32), jax.ShapeDtypeStruct((t, D), BF16)),
        grid=(t // tm,),
        in_specs=[
            pl.BlockSpec((tm, D), lambda i: (i, 0)),
            pl.BlockSpec((None, D, D), lambda i: (layer, 0, 0), pipeline_mode=pl.Buffered(1)),
            pl.BlockSpec((tm, D), lambda i: (i, 0)),
            _mod_spec(layer, 2, row_of),
            pl.BlockSpec((1, D), lambda i: (0, 0)),
            _mod_spec(layer, 3, row_of),
            _mod_spec(layer, 4, row_of),
        ],
        out_specs=(pl.BlockSpec((tm, D), lambda i: (i, 0)),
                   pl.BlockSpec((tm, D), lambda i: (i, 0))),
        scratch_shapes=[pltpu.VMEM((D, D), BF16)],
        compiler_params=_cp(("arbitrary",), 56),
        name="out_proj",
    )(m, w_o, x, mods5, g_ffn.reshape(1, D), mods5, mods5)


FF_BLOCKS = D_FF // LANES
FF_SUBW = 2 * LANES
FF_NSUB = 2
TN_FF = FF_NSUB * FF_SUBW
FF_NBLK = TN_FF // LANES
FF_MCHUNK = 256


def _chunked_dot(h_ref, w):
    parts = [jnp.dot(h_ref[r0:r0 + FF_MCHUNK, :], w, preferred_element_type=F32)
             for r0 in range(0, h_ref.shape[0], FF_MCHUNK)]
    return jnp.concatenate(parts, axis=0)


def _ffn_up_kernel(h_ref, *refs, seq_len):
    a_refs = refs[0:FF_NBLK]
    b_refs = refs[FF_NBLK:2 * FF_NBLK]
    ca_refs = refs[2 * FF_NBLK:3 * FF_NBLK]
    cb_refs = refs[3 * FF_NBLK:4 * FF_NBLK]
    o_ref = refs[4 * FF_NBLK]
    ws = []
    for s in range(FF_NSUB):
        k0 = s * (FF_SUBW // LANES)
        k1 = k0 + FF_SUBW // LANES
        ws.append((jnp.concatenate([r[...] for r in a_refs[k0:k1]], axis=1).astype(BF16),
                   jnp.concatenate([r[...] for r in b_refs[k0:k1]], axis=1).astype(BF16),
                   jnp.concatenate([r[...] for r in ca_refs[k0:k1]], axis=1),
                   jnp.concatenate([r[...] for r in cb_refs[k0:k1]], axis=1)))
    for s in range(FF_NSUB):
        wa, wb, ca, cb = ws[s]
        a = _dwconv3(_chunked_dot(h_ref, wa), ca, seq_len)
        b = _dwconv3(_chunked_dot(h_ref, wb), cb, seq_len)
        o_ref[:, s * FF_SUBW:(s + 1) * FF_SUBW] = (a * _sigmoid(a) * b).astype(o_ref.dtype)


def _ffn_up(h, w_up, w_conv, layer, seq_len, rows):
    t = h.shape[0]
    nj = pl.cdiv(D_FF, TN_FF)
    last = 2 * FF_BLOCKS - 1

    def wspec(rows_, off, k):
        return pl.BlockSpec((None, rows_, LANES),
                            lambda i, j: (layer, 0, jnp.minimum(off + FF_NBLK * j + k, last)))

    blocks = range(FF_NBLK)
    specs = ([wspec(D, 0, k) for k in blocks] + [wspec(D, FF_BLOCKS, k) for k in blocks]
             + [wspec(3, 0, k) for k in blocks] + [wspec(3, FF_BLOCKS, k) for k in blocks])
    return pl.pallas_call(
        functools.partial(_ffn_up_kernel, seq_len=seq_len),
        out_shape=jax.ShapeDtypeStruct((t, D_FF), BF16),
        grid=(t // rows, nj),
        in_specs=[pl.BlockSpec((rows, D), lambda i, j: (i, 0))] + specs,
        out_specs=pl.BlockSpec((rows, TN_FF), lambda i, j: (i, j)),
        compiler_params=_cp(("arbitrary", "arbitrary"), 56),
        name="ffn_up",
    )(h, *([w_up] * (2 * FF_NBLK)), *([w_conv] * (2 * FF_NBLK)))


def _ffn_down_kernel(g_ref, w_ref, x_ref, gt_ref, o_ref, w_s):
    @pl.when(pl.program_id(1) == 0)
    def _():
        w_s[...] = w_ref[...].astype(BF16)

    y = jnp.dot(g_ref[...], w_s[...], preferred_element_type=F32)
    o_ref[...] = x_ref[...] + gt_ref[...] * y


def _ffn_down(g, w_down, x, mods5, layer, ctx):
    t = g.shape[0]
    tm = t if ctx else 512
    tn = 512
    row_of = _row_of(ctx, SEQ // tm, 1)
    lhs_mode = {"pipeline_mode": pl.Buffered(1)} if ctx else {}
    return pl.pallas_call(
        _ffn_down_kernel,
        out_shape=jax.ShapeDtypeStruct((t, D), F32),
        grid=(D // tn, t // tm),
        in_specs=[
            pl.BlockSpec((tm, D_FF), lambda j, i: (i, 0), **lhs_mode),
            pl.BlockSpec((None, D_FF, tn), lambda j, i: (layer, 0, j)),
            pl.BlockSpec((tm, tn), lambda j, i: (i, j)),
            _mod_spec(layer, 5, row_of, tn, lambda j, i: j),
        ],
        out_specs=pl.BlockSpec((tm, tn), lambda j, i: (i, j)),
        scratch_shapes=[pltpu.VMEM((D_FF, tn), BF16)],
        compiler_params=_cp(("arbitrary", "arbitrary"), 54),
        name="ffn_down",
    )(g, w_down, x, mods5)


def kernel(x, c, ctx, c_ctx, w_ada, b_ada, g_mix, w_in, g_q, g_k, w_gmlp, b_gmlp, g_gmlp_v,
           w_sconv, w_pa, w_pb, w_pc, w_o, g_ffn, w_up, w_ffn_conv, w_down, g_final):
    cvec = jnp.concatenate([c, c_ctx[None, :], jnp.zeros((3, D), F32)], axis=0)
    mods5 = _mods(cvec, w_ada, b_ada).reshape(DEPTH, 8, 6, 1, D)
    rope_tabs = _rope_tables()
    q_scale = HD ** -0.5 * math.log2(math.e)

    xs = x.reshape(BATCH * SEQ, D)
    cs = ctx.reshape(BATCH * CTX, D)

    for l in range(DEPTH):
        last = l == DEPTH - 1
        bs = jnp.broadcast_to(b_gmlp[l][:, :, None], (GG, CHUNK, CHUNK))
        hx = _normmod(xs, g_mix[l], mods5, l, False)
        hc = _normmod(cs, g_mix[l], mods5, l, True)

        raw_x = _inproj(hx, w_in, l, COLS_RAW, "raw")
        zg_x = _inproj(hx, w_in, l, COLS_GELU, "gelu")
        gates_x = _inproj(hx, w_in, l, COLS_GATE, "sigmoid")
        if not last:
            raw_c = _inproj(hc, w_in, l, COLS_RAW, "raw")
            kc = _headnorm(raw_c, R_K // KV_DIM, N_KV, g_k[l], 1.0, None)
            ctx_kv = (kc, raw_c, R_V // HD)
        else:
            kv_c = _inproj(hc, w_in, l, COLS_KV, "raw")
            kc = _headnorm(kv_c, 0, N_KV, g_k[l], 1.0, None)
            ctx_kv = (kc, kv_c, KV_DIM // HD)
        qx = _headnorm(raw_x, 0, N_HEADS, g_q[l], q_scale, rope_tabs)
        kx = _headnorm(raw_x, R_K // KV_DIM, N_KV, g_k[l], 1.0, rope_tabs)
        attn_x = _attention(qx, SEQ, (kx, raw_x, R_V // HD), ctx_kv)
        ya = _gmlp(zg_x, g_gmlp_v[l], w_gmlp[l], bs)
        yb = _sconv(raw_x, w_sconv[l], SEQ, SEQ)
        m = _merge(ya, yb, attn_x, gates_x, w_pa, w_pb, w_pc, l)
        x1, fx = _outproj(m, w_o, xs, g_ffn[l], mods5, l, False)

        gx = _ffn_up(fx, w_up, w_ffn_conv, l, SEQ, SEQ)
        xs = _ffn_down(gx, w_down, x1, mods5, l, False)

        if not last:
            zg_c = _inproj(hc, w_in, l, COLS_GELU, "gelu")
            gates_c = _inproj(hc, w_in, l, COLS_GATE, "sigmoid")
            qc = _headnorm(raw_c, 0, N_HEADS, g_q[l], q_scale, None)
            attn_c = _attention(qc, CTX, None, ctx_kv)
            yac = _gmlp(zg_c, g_gmlp_v[l], w_gmlp[l], bs)
            ybc = _sconv(raw_c, w_sconv[l], CTX, BATCH * CTX)
            mc = _merge(yac, ybc, attn_c, gates_c, w_pa, w_pb, w_pc, l)
            c1, fc = _outproj(mc, w_o, cs, g_ffn[l], mods5, l, True)
            gc = _ffn_up(fc, w_up, w_ffn_conv, l, CTX, BATCH * CTX)
            cs = _ffn_down(gc, w_down, c1, mods5, l, True)

    return _final_norm(xs, g_final).reshape(BATCH, SEQ, D)
```

```python
import functools
import math

import jax
import jax.numpy as jnp
from jax.experimental import pallas as pl
from jax.experimental.pallas import tpu as pltpu

F32 = jnp.float32
BF16 = jnp.bfloat16

D = 2048
BATCH = 4
SEQ = 2048
CTX = 256
DEPTH = 2
GRID_W = 64
N_HEADS = 16
N_KV = 4
HD = 128
GROUP = N_HEADS // N_KV
CHUNK = 128
GW = D // 2
GG = 8
CW = D // 2
D_FF = 5504
EPS = 1e-6
ROPE_THETA = 10000.0
KV_DIM = N_KV * HD
IN_DIM = D + 2 * KV_DIM + 2 * GW + 3 * CW + 3 * D
OFF_K = D
OFF_U = D + 2 * KV_DIM
OFF_B = OFF_U + 2 * GW
OFF_G = OFF_B + 3 * CW
R_K = D
R_V = D + KV_DIM
R_B = D + 2 * KV_DIM
R_C = R_B + CW
R_H = R_C + CW
LANES = 128

VMEM_PHYS_V7X = 64 * 1024 * 1024
VMEM_CAP = VMEM_PHYS_V7X - 8 * 1024 * 1024


def _cp(sem, vmem_mb):
    return pltpu.CompilerParams(
        dimension_semantics=sem,
        vmem_limit_bytes=min(int(vmem_mb * 1024 * 1024), VMEM_CAP))


def _rms(xf, g):
    ms = jnp.mean(xf * xf, axis=-1, keepdims=True)
    return xf * jax.lax.rsqrt(ms + EPS) * g


def _sigmoid(x):
    return 0.5 * jnp.tanh(0.5 * x) + 0.5


def _gelu_tanh(x):
    c = math.sqrt(2.0 / math.pi)
    return 0.5 * x * (1.0 + jnp.tanh(c * (x + 0.044715 * (x * x * x))))


def _mods_kernel(c_ref, w_ref, b_ref, o_ref):
    c = c_ref[...]
    s = (c * _sigmoid(c)).astype(BF16)
    w = w_ref[...].astype(BF16)
    o_ref[...] = jnp.dot(s, w, preferred_element_type=F32) + b_ref[...]


def _mods(cvec, w_ada, b_ada):
    tn = 2048
    nj = 6 * D // tn
    return pl.pallas_call(
        _mods_kernel,
        out_shape=jax.ShapeDtypeStruct((DEPTH, 8, 6 * D), F32),
        grid=(DEPTH, nj),
        in_specs=[
            pl.BlockSpec((8, D), lambda l, j: (0, 0)),
            pl.BlockSpec((None, D, tn), lambda l, j: (l, 0, j)),
            pl.BlockSpec((None, 1, tn), lambda l, j: (l, 0, j)),
        ],
        out_specs=pl.BlockSpec((None, 8, tn), lambda l, j: (l, 0, j)),
        compiler_params=_cp(("arbitrary", "arbitrary"), 52),
        name="adaln_mods",
    )(cvec, w_ada, b_ada.reshape(DEPTH, 1, 6 * D))


def _mod_spec(layer, k, row_of, width=D, col_of=None):
    if col_of is None:
        return pl.BlockSpec((None, None, None, 1, width),
                            lambda *g: (layer, row_of(*g), k, 0, 0))
    return pl.BlockSpec((None, None, None, 1, width),
                        lambda *g: (layer, row_of(*g), k, 0, col_of(*g)))


def _row_of(ctx, tiles_per_batch, axis):
    if ctx:
        return lambda *g: BATCH
    return lambda *g: g[axis] // tiles_per_batch


def _normmod_kernel(x_ref, g_ref, sh_ref, sc_ref, o_ref):
    y = _rms(x_ref[...], g_ref[...])
    o_ref[...] = (y * (1.0 + sc_ref[...]) + sh_ref[...]).astype(o_ref.dtype)


def _norm_kernel(x_ref, g_ref, o_ref):
    o_ref[...] = _rms(x_ref[...], g_ref[...]).astype(o_ref.dtype)


def _normmod(x, g, mods5, layer, ctx):
    t = x.shape[0]
    tm = 1024
    row_of = _row_of(ctx, SEQ // tm, 0)
    return pl.pallas_call(
        _normmod_kernel,
        out_shape=jax.ShapeDtypeStruct((t, D), BF16),
        grid=(t // tm,),
        in_specs=[
            pl.BlockSpec((tm, D), lambda i: (i, 0)),
            pl.BlockSpec((1, D), lambda i: (0, 0)),
            _mod_spec(layer, 0, row_of),
            _mod_spec(layer, 1, row_of),
        ],
        out_specs=pl.BlockSpec((tm, D), lambda i: (i, 0)),
        compiler_params=_cp(("arbitrary",), 44),
        name="norm_mod",
    )(x, g.reshape(1, D), mods5, mods5)


def _final_norm(x, g):
    t = x.shape[0]
    tm = 1024
    return pl.pallas_call(
        _norm_kernel,
        out_shape=jax.ShapeDtypeStruct((t, D), F32),
        grid=(t // tm,),
        in_specs=[
            pl.BlockSpec((tm, D), lambda i: (i, 0)),
            pl.BlockSpec((1, D), lambda i: (0, 0)),
        ],
        out_specs=pl.BlockSpec((tm, D), lambda i: (i, 0)),
        compiler_params=_cp(("arbitrary",), 48),
        name="final_norm",
    )(x, g.reshape(1, D))


TN_IN = 1024
IN_MCHUNK = {"raw": 1024, "sigmoid": 1024, "gelu": 512}
_ACTS = {"raw": lambda a: a, "gelu": _gelu_tanh, "sigmoid": _sigmoid}


def _inproj_kernel(h_ref, w_ref, o_ref, wb_ref, *, act):
    @pl.when(pl.program_id(1) == 0)
    def _():
        wb_ref[...] = w_ref[...].astype(BF16)

    w = wb_ref[...]
    mc = min(IN_MCHUNK[act], h_ref.shape[0])
    for r0 in range(0, h_ref.shape[0], mc):
        acc = jnp.dot(h_ref[r0:r0 + mc, :], w, preferred_element_type=F32)
        o_ref[r0:r0 + mc, :] = _ACTS[act](acc).astype(o_ref.dtype)


def _inproj(h, w, layer, col_tiles, act):
    t = h.shape[0]
    tm = min(t, 2048)
    first, nj, skip_at, skip = col_tiles

    def wmap(j, i):
        return (layer, 0, first + j + skip * (j // skip_at))

    return pl.pallas_call(
        functools.partial(_inproj_kernel, act=act),
        out_shape=jax.ShapeDtypeStruct((t, nj * TN_IN), BF16),
        grid=(nj, t // tm),
        in_specs=[
            pl.BlockSpec((tm, D), lambda j, i: (i, 0)),
            pl.BlockSpec((None, D, TN_IN), wmap),
        ],
        out_specs=pl.BlockSpec((tm, TN_IN), lambda j, i: (i, j)),
        scratch_shapes=[pltpu.VMEM((D, TN_IN), BF16)],
        compiler_params=_cp(("arbitrary", "arbitrary"), 56),
        name="in_proj_" + act,
    )(h, w)


COLS_RAW = (0, 6, 3, (OFF_B - OFF_U) // TN_IN)
COLS_GELU = (OFF_U // TN_IN, (OFF_B - OFF_U) // TN_IN, 1 << 20, 0)
COLS_GATE = (OFF_G // TN_IN, 3 * D // TN_IN, 1 << 20, 0)
COLS_KV = (OFF_K // TN_IN, 1, 1 << 20, 0)


def _rope_tables():
    rows = SEQ // GRID_W
    row = jnp.repeat(jnp.arange(rows, dtype=F32), GRID_W)
    col = jnp.tile(jnp.arange(GRID_W, dtype=F32), rows)
    n_freq = HD // 4
    inv_freq = ROPE_THETA ** (-jnp.arange(n_freq, dtype=F32) / n_freq)
    ar = row[:, None] * inv_freq
    ac = col[:, None] * inv_freq
    cos = jnp.concatenate([jnp.cos(ar), jnp.cos(ar), jnp.cos(ac), jnp.cos(ac)], axis=-1)
    sin = jnp.concatenate([-jnp.sin(ar), jnp.sin(ar), -jnp.sin(ac), jnp.sin(ac)], axis=-1)
    return cos, sin


def _headnorm_kernel(x_ref, g_ref, *rest, n_heads, scale, rope):
    def first_half_mask(rows):
        lane = jax.lax.broadcasted_iota(jnp.int32, (rows, HD), 1)
        return (lane % (HD // 2)) < (HD // 4)

    def swap_halves(v, mask):
        return jnp.where(mask, pltpu.roll(v, HD - HD // 4, axis=1), pltpu.roll(v, HD // 4, axis=1))

    g = g_ref[...] * scale
    if rope:
        cos_ref, sin_ref, o_ref = rest
        g_partner = swap_halves(jnp.broadcast_to(g, (8, HD)), first_half_mask(8))[0:1, :]
        gcos = cos_ref[...] * g
        gsin = sin_ref[...] * g_partner
        packed_mask = first_half_mask(x_ref.shape[0] // 2)
    else:
        (o_ref,) = rest
    for h in range(n_heads):
        xb = x_ref[:, h * HD:(h + 1) * HD]
        x = xb.astype(F32)
        if rope:
            rstd = jax.lax.rsqrt(jnp.mean(x * x, axis=-1, keepdims=True) + EPS)
            packed = pltpu.bitcast(xb, jnp.uint32)
            partner = pltpu.bitcast(swap_halves(packed, packed_mask), BF16).astype(F32)
            y = (x * gcos + partner * gsin) * rstd
        else:
            y = _rms(x, g)
        o_ref[:, h * HD:(h + 1) * HD] = y.astype(o_ref.dtype)


def _headnorm(src, col_blk, n_heads, g, scale, rope_tabs):
    t = src.shape[0]
    tm = 1024
    w = n_heads * HD
    rope = rope_tabs is not None
    in_specs = [
        pl.BlockSpec((tm, w), lambda i: (i, col_blk)),
        pl.BlockSpec((1, HD), lambda i: (0, 0)),
    ]
    args = [src, g.reshape(1, HD)]
    if rope:
        nt = SEQ // tm
        in_specs += [pl.BlockSpec((tm, HD), lambda i: (i % nt, 0))] * 2
        args += list(rope_tabs)
    return pl.pallas_call(
        functools.partial(_headnorm_kernel, n_heads=n_heads, scale=scale, rope=rope),
        out_shape=jax.ShapeDtypeStruct((t, w), BF16),
        grid=(t // tm,),
        in_specs=in_specs,
        out_specs=pl.BlockSpec((tm, w), lambda i: (i, 0)),
        compiler_params=_cp(("arbitrary",), 44),
        name="head_norm",
    )(*args)


TQ_SUB = 512
KEY_TILE = 256


def _attn_kernel(q_ref, *refs, with_latent):
    if with_latent:
        kx_ref, vx_ref, kc_ref, vc_ref, o_ref, k_s, vt_s = refs
        k_s[0:SEQ, :] = kx_ref[...]
        k_s[SEQ:SEQ + CTX, :] = kc_ref[...]
        vt_s[:, 0:SEQ] = vx_ref[...].T
        vt_s[:, SEQ:SEQ + CTX] = vc_ref[...].T
    else:
        kc_ref, vc_ref, o_ref, k_s, vt_s = refs
        k_s[...] = kc_ref[...]
        vt_s[...] = vc_ref[...].T
    tiles = list(range(0, k_s.shape[0], KEY_TILE))
    tqs = min(TQ_SUB, q_ref.shape[0])
    units = [(r0, g0) for r0 in range(0, q_ref.shape[0], tqs) for g0 in range(0, GROUP, 2)]

    def queries(u):
        r0, g0 = units[u]
        return jnp.concatenate([q_ref[r0:r0 + tqs, g0 * HD:(g0 + 1) * HD],
                                q_ref[r0:r0 + tqs, (g0 + 1) * HD:(g0 + 2) * HD]], axis=0)

    def scores(q, t0):
        return jax.lax.dot_general(k_s[t0:t0 + KEY_TILE, :], q, (((1,), (1,)), ((), ())),
                                   preferred_element_type=F32)

    def colmax(sts):
        m = None
        for st in sts:
            mc = jnp.max(st, axis=0, keepdims=True)
            m = mc if m is None else jnp.maximum(m, mc)
        return m

    sts = [scores(queries(0), t0) for t0 in tiles]
    m = colmax(sts)
    for u, (r0, g0) in enumerate(units):
        q_next = queries(u + 1) if u + 1 < len(units) else None
        sts_next = []
        l = None
        ot = None
        for t0, st in zip(tiles, sts):
            p = jnp.exp2(st - m)
            lc = jnp.sum(p, axis=0, keepdims=True)
            oc = jnp.dot(vt_s[:, t0:t0 + KEY_TILE], p.astype(BF16),
                         preferred_element_type=F32)
            l = lc if l is None else l + lc
            ot = oc if ot is None else ot + oc
            if q_next is not None:
                sts_next.append(scores(q_next, t0))
        ot = ot * (1.0 / l)
        o_ref[r0:r0 + tqs, g0 * HD:(g0 + 1) * HD] = ot[:, :tqs].T.astype(o_ref.dtype)
        o_ref[r0:r0 + tqs, (g0 + 1) * HD:(g0 + 2) * HD] = ot[:, tqs:].T.astype(o_ref.dtype)
        if q_next is not None:
            sts = sts_next
            m = colmax(sts)


def _attention(q, lq, latent, context):
    with_latent = latent is not None
    s_len = (SEQ if with_latent else 0) + CTX
    tq = min(lq, 2048)
    nq = lq // tq
    in_specs = [pl.BlockSpec((tq, GROUP * HD), lambda b, h, i: (b * nq + i, h))]
    args = [q]
    for src, rows in ((latent, SEQ), (context, CTX)):
        if src is None:
            continue
        k_arr, v_arr, v_blk = src
        in_specs += [pl.BlockSpec((rows, HD), lambda b, h, i: (b, h)),
                     pl.BlockSpec((rows, HD), lambda b, h, i, v_blk=v_blk: (b, v_blk + h))]
        args += [k_arr, v_arr]
    return pl.pallas_call(
        functools.partial(_attn_kernel, with_latent=with_latent),
        out_shape=jax.ShapeDtypeStruct((BATCH * lq, D), BF16),
        grid=(BATCH, N_KV, nq),
        in_specs=in_specs,
        out_specs=pl.BlockSpec((tq, GROUP * HD), lambda b, h, i: (b * nq + i, h)),
        scratch_shapes=[pltpu.VMEM((s_len, HD), BF16), pltpu.VMEM((HD, s_len), BF16)],
        compiler_params=_cp(("arbitrary", "arbitrary", "arbitrary"), 48),
        name="attention",
    )(*args)


def _gmlp_kernel(u_ref, v_ref, gv_ref, ws_ref, bs_ref, o_ref, *, tm):
    v = _rms(v_ref[...].astype(F32), gv_ref[...]).astype(BF16)
    for g in range(GG):
        c0 = g * CHUNK
        w = ws_ref[g].astype(BF16)
        for n in range(tm // CHUNK):
            r0 = n * CHUNK
            mixed = jnp.dot(w, v[r0:r0 + CHUNK, c0:c0 + CHUNK],
                            preferred_element_type=F32) + bs_ref[g]
            u = u_ref[r0:r0 + CHUNK, c0:c0 + CHUNK].astype(F32)
            o_ref[r0:r0 + CHUNK, c0:c0 + CHUNK] = (u * mixed).astype(o_ref.dtype)


def _gmlp(zg, gv, ws, bs):
    t = zg.shape[0]
    tm = 1024
    return pl.pallas_call(
        functools.partial(_gmlp_kernel, tm=tm),
        out_shape=jax.ShapeDtypeStruct((t, GW), BF16),
        grid=(t // tm,),
        in_specs=[
            pl.BlockSpec((tm, GW), lambda i: (i, 0)),
            pl.BlockSpec((tm, GW), lambda i: (i, 1)),
            pl.BlockSpec((1, GW), lambda i: (0, 0)),
            pl.BlockSpec((GG, CHUNK, CHUNK), lambda i: (0, 0, 0)),
            pl.BlockSpec((GG, CHUNK, CHUNK), lambda i: (0, 0, 0)),
        ],
        out_specs=pl.BlockSpec((tm, GW), lambda i: (i, 0)),
        compiler_params=_cp(("arbitrary",), 32),
        name="gmlp",
    )(zg, zg, gv.reshape(1, GW), ws, bs)


def _dwconv3(x, w, seq_len):
    rows = x.shape[0]
    pos = jax.lax.broadcasted_iota(jnp.int32, x.shape, 0) % seq_len
    prev = jnp.where(pos == 0, 0.0, pltpu.roll(x, 1, axis=0))
    nxt = jnp.where(pos == seq_len - 1, 0.0, pltpu.roll(x, rows - 1, axis=0))
    return prev * w[0:1, :] + x * w[1:2, :] + nxt * w[2:3, :]


def _sconv_kernel(b_ref, c_ref, h_ref, w_ref, o_ref, *, seq_len):
    t = c_ref[...].astype(F32) * h_ref[...].astype(F32)
    y = b_ref[...].astype(F32) * _dwconv3(t, w_ref[...], seq_len)
    o_ref[...] = y.astype(o_ref.dtype)


def _sconv(raw, w, seq_len, rows):
    t = raw.shape[0]
    tc = 512
    return pl.pallas_call(
        functools.partial(_sconv_kernel, seq_len=seq_len),
        out_shape=jax.ShapeDtypeStruct((t, CW), BF16),
        grid=(t // rows, CW // tc),
        in_specs=[
            pl.BlockSpec((rows, tc), lambda i, j: (i, R_B // tc + j)),
            pl.BlockSpec((rows, tc), lambda i, j: (i, R_C // tc + j)),
            pl.BlockSpec((rows, tc), lambda i, j: (i, R_H // tc + j)),
            pl.BlockSpec((3, tc), lambda i, j: (0, j)),
        ],
        out_specs=pl.BlockSpec((rows, tc), lambda i, j: (i, j)),
        compiler_params=_cp(("arbitrary", "arbitrary"), 40),
        name="short_conv",
    )(raw, raw, raw, w)


def _merge_kernel(ya_ref, yb_ref, at_ref, wa_ref, wb_ref, wc_ref,
                  ga_ref, gb_ref, gc_ref, o_ref, wa_s, wb_s, wc_s):
    @pl.when(pl.program_id(1) == 0)
    def _():
        wa_s[...] = wa_ref[...].astype(BF16)
        wb_s[...] = wb_ref[...].astype(BF16)
        wc_s[...] = wc_ref[...].astype(BF16)

    a = jnp.dot(ya_ref[...], wa_s[...], preferred_element_type=F32)
    m = ga_ref[...].astype(F32) * a
    b = jnp.dot(yb_ref[...], wb_s[...], preferred_element_type=F32)
    m = m + gb_ref[...].astype(F32) * b
    c = jnp.dot(at_ref[...], wc_s[...], preferred_element_type=F32)
    m = m + gc_ref[...].astype(F32) * c
    o_ref[...] = m.astype(o_ref.dtype)


def _merge(ya, yb, attn, gates, w_pa, w_pb, w_pc, layer):
    t = ya.shape[0]
    tm = 1024
    tn = 512
    nb = D // tn
    return pl.pallas_call(
        _merge_kernel,
        out_shape=jax.ShapeDtypeStruct((t, D), BF16),
        grid=(nb, t // tm),
        in_specs=[
            pl.BlockSpec((tm, GW), lambda j, i: (i, 0)),
            pl.BlockSpec((tm, CW), lambda j, i: (i, 0)),
            pl.BlockSpec((tm, D), lambda j, i: (i, 0)),
            pl.BlockSpec((None, GW, tn), lambda j, i: (layer, 0, j)),
            pl.BlockSpec((None, CW, tn), lambda j, i: (layer, 0, j)),
            pl.BlockSpec((None, D, tn), lambda j, i: (layer, 0, j)),
            pl.BlockSpec((tm, tn), lambda j, i: (i, j)),
            pl.BlockSpec((tm, tn), lambda j, i: (i, nb + j)),
            pl.BlockSpec((tm, tn), lambda j, i: (i, 2 * nb + j)),
        ],
        out_specs=pl.BlockSpec((tm, tn), lambda j, i: (i, j)),
        scratch_shapes=[pltpu.VMEM((GW, tn), BF16), pltpu.VMEM((CW, tn), BF16),
                        pltpu.VMEM((D, tn), BF16)],
        compiler_params=_cp(("arbitrary", "arbitrary"), 56),
        name="merge",
    )(ya, yb, attn, w_pa, w_pb, w_pc, gates, gates, gates)


def _outproj_kernel(m_ref, w_ref, x_ref, gt_ref, g_ref, sh_ref, sc_ref, x1_ref, h_ref, w_s):
    @pl.when(pl.program_id(0) == 0)
    def _():
        w_s[...] = w_ref[...].astype(BF16)

    y = jnp.dot(m_ref[...], w_s[...], preferred_element_type=F32)
    x1 = x_ref[...] + gt_ref[...] * y
    x1_ref[...] = x1
    h_ref[...] = (_rms(x1, g_ref[...]) * (1.0 + sc_ref[...]) + sh_ref[...]).astype(h_ref.dtype)


def _outproj(m, w_o, x, g_ffn, mods5, layer, ctx):
    t = m.shape[0]
    tm = 512
    row_of = _row_of(ctx, SEQ // tm, 0)
    return pl.pallas_call(
        _outproj_kernel,
        out_shape=(jax.ShapeDtypeStruct((t, D), F32), jax.ShapeDtypeStruct((t, D), BF16)),
        grid=(t // tm,),
        in_specs=[
            pl.BlockSpec((tm, D), lambda i: (i, 0)),
            pl.BlockSpec((None, D, D), lambda i: (layer, 0, 0), pipeline_mode=pl.Buffered(1)),
            pl.BlockSpec((tm, D), lambda i: (i, 0)),
            _mod_spec(layer, 2, row_of),
            pl.BlockSpec((1, D), lambda i: (0, 0)),
            _mod_spec(layer, 3, row_of),
            _mod_spec(layer, 4, row_of),
        ],
        out_specs=(pl.BlockSpec((tm, D), lambda i: (i, 0)),
                   pl.BlockSpec((tm, D), lambda i: (i, 0))),
        scratch_shapes=[pltpu.VMEM((D, D), BF16)],
        compiler_params=_cp(("arbitrary",), 56),
        name="out_proj",
    )(m, w_o, x, mods5, g_ffn.reshape(1, D), mods5, mods5)


FF_BLOCKS = D_FF // LANES
FF_SUBW = 4 * LANES
FF_NSUB = 1
TN_FF = FF_NSUB * FF_SUBW
FF_NBLK = TN_FF // LANES
FF_MCHUNK = 256


def _chunked_dot(h_ref, w):
    parts = [jnp.dot(h_ref[r0:r0 + FF_MCHUNK, :], w, preferred_element_type=F32)
             for r0 in range(0, h_ref.shape[0], FF_MCHUNK)]
    return jnp.concatenate(parts, axis=0)


def _ffn_up_kernel(h_ref, *refs, seq_len):
    a_refs = refs[0:FF_NBLK]
    b_refs = refs[FF_NBLK:2 * FF_NBLK]
    ca_refs = refs[2 * FF_NBLK:3 * FF_NBLK]
    cb_refs = refs[3 * FF_NBLK:4 * FF_NBLK]
    o_ref = refs[4 * FF_NBLK]
    ws = []
    for s in range(FF_NSUB):
        k0 = s * (FF_SUBW // LANES)
        k1 = k0 + FF_SUBW // LANES
        ws.append((jnp.concatenate([r[...] for r in a_refs[k0:k1]], axis=1).astype(BF16),
                   jnp.concatenate([r[...] for r in b_refs[k0:k1]], axis=1).astype(BF16),
                   jnp.concatenate([r[...] for r in ca_refs[k0:k1]], axis=1),
                   jnp.concatenate([r[...] for r in cb_refs[k0:k1]], axis=1)))
    for s in range(FF_NSUB):
        wa, wb, ca, cb = ws[s]
        a = _dwconv3(_chunked_dot(h_ref, wa), ca, seq_len)
        b = _dwconv3(_chunked_dot(h_ref, wb), cb, seq_len)
        o_ref[:, s * FF_SUBW:(s + 1) * FF_SUBW] = (a * _sigmoid(a) * b).astype(o_ref.dtype)


def _ffn_up(h, w_up, w_conv, layer, seq_len, rows):
    t = h.shape[0]
    nj = pl.cdiv(D_FF, TN_FF)
    last = 2 * FF_BLOCKS - 1

    def wspec(rows_, off, k):
        return pl.BlockSpec((None, rows_, LANES),
                            lambda i, j: (layer, 0, jnp.minimum(off + FF_NBLK * j + k, last)))

    blocks = range(FF_NBLK)
    specs = ([wspec(D, 0, k) for k in blocks] + [wspec(D, FF_BLOCKS, k) for k in blocks]
             + [wspec(3, 0, k) for k in blocks] + [wspec(3, FF_BLOCKS, k) for k in blocks])
    return pl.pallas_call(
        functools.partial(_ffn_up_kernel, seq_len=seq_len),
        out_shape=jax.ShapeDtypeStruct((t, D_FF), BF16),
        grid=(t // rows, nj),
        in_specs=[pl.BlockSpec((rows, D), lambda i, j: (i, 0))] + specs,
        out_specs=pl.BlockSpec((rows, TN_FF), lambda i, j: (i, j)),
        compiler_params=_cp(("arbitrary", "arbitrary"), 56),
        name="ffn_up",
    )(h, *([w_up] * (2 * FF_NBLK)), *([w_conv] * (2 * FF_NBLK)))


def _ffn_down_kernel(g_ref, w_ref, x_ref, gt_ref, o_ref, w_s):
    @pl.when(pl.program_id(1) == 0)
    def _():
        w_s[...] = w_ref[...].astype(BF16)

    y = jnp.dot(g_ref[...], w_s[...], preferred_element_type=F32)
    o_ref[...] = x_ref[...] + gt_ref[...] * y


def _ffn_down(g, w_down, x, mods5, layer, ctx):
    t = g.shape[0]
    tm = t if ctx else 512
    tn = 512
    row_of = _row_of(ctx, SEQ // tm, 1)
    lhs_mode = {"pipeline_mode": pl.Buffered(1)} if ctx else {}
    return pl.pallas_call(
        _ffn_down_kernel,
        out_shape=jax.ShapeDtypeStruct((t, D), F32),
        grid=(D // tn, t // tm),
        in_specs=[
            pl.BlockSpec((tm, D_FF), lambda j, i: (i, 0), **lhs_mode),
            pl.BlockSpec((None, D_FF, tn), lambda j, i: (layer, 0, j)),
            pl.BlockSpec((tm, tn), lambda j, i: (i, j)),
            _mod_spec(layer, 5, row_of, tn, lambda j, i: j),
        ],
        out_specs=pl.BlockSpec((tm, tn), lambda j, i: (i, j)),
        scratch_shapes=[pltpu.VMEM((D_FF, tn), BF16)],
        compiler_params=_cp(("arbitrary", "arbitrary"), 54),
        name="ffn_down",
    )(g, w_down, x, mods5)


def kernel(x, c, ctx, c_ctx, w_ada, b_ada, g_mix, w_in, g_q, g_k, w_gmlp, b_gmlp, g_gmlp_v,
           w_sconv, w_pa, w_pb, w_pc, w_o, g_ffn, w_up, w_ffn_conv, w_down, g_final):
    cvec = jnp.concatenate([c, c_ctx[None, :], jnp.zeros((3, D), F32)], axis=0)
    mods5 = _mods(cvec, w_ada, b_ada).reshape(DEPTH, 8, 6, 1, D)
    rope_tabs = _rope_tables()
    q_scale = HD ** -0.5 * math.log2(math.e)

    xs = x.reshape(BATCH * SEQ, D)
    cs = ctx.reshape(BATCH * CTX, D)

    for l in range(DEPTH):
        last = l == DEPTH - 1
        bs = jnp.broadcast_to(b_gmlp[l][:, :, None], (GG, CHUNK, CHUNK))
        hx = _normmod(xs, g_mix[l], mods5, l, False)
        hc = _normmod(cs, g_mix[l], mods5, l, True)

        raw_x = _inproj(hx, w_in, l, COLS_RAW, "raw")
        zg_x = _inproj(hx, w_in, l, COLS_GELU, "gelu")
        gates_x = _inproj(hx, w_in, l, COLS_GATE, "sigmoid")
        if not last:
            raw_c = _inproj(hc, w_in, l, COLS_RAW, "raw")
            kc = _headnorm(raw_c, R_K // KV_DIM, N_KV, g_k[l], 1.0, None)
            ctx_kv = (kc, raw_c, R_V // HD)
        else:
            kv_c = _inproj(hc, w_in, l, COLS_KV, "raw")
            kc = _headnorm(kv_c, 0, N_KV, g_k[l], 1.0, None)
            ctx_kv = (kc, kv_c, KV_DIM // HD)
        qx = _headnorm(raw_x, 0, N_HEADS, g_q[l], q_scale, rope_tabs)
        kx = _headnorm(raw_x, R_K // KV_DIM, N_KV, g_k[l], 1.0, rope_tabs)
        attn_x = _attention(qx, SEQ, (kx, raw_x, R_V // HD), ctx_kv)
        ya = _gmlp(zg_x, g_gmlp_v[l], w_gmlp[l], bs)
        yb = _sconv(raw_x, w_sconv[l], SEQ, SEQ)
        m = _merge(ya, yb, attn_x, gates_x, w_pa, w_pb, w_pc, l)
        x1, fx = _outproj(m, w_o, xs, g_ffn[l], mods5, l, False)

        gx = _ffn_up(fx, w_up, w_ffn_conv, l, SEQ, SEQ)
        xs = _ffn_down(gx, w_down, x1, mods5, l, False)

        if not last:
            zg_c = _inproj(hc, w_in, l, COLS_GELU, "gelu")
            gates_c = _inproj(hc, w_in, l, COLS_GATE, "sigmoid")
            qc = _headnorm(raw_c, 0, N_HEADS, g_q[l], q_scale, None)
            attn_c = _attention(qc, CTX, None, ctx_kv)
            yac = _gmlp(zg_c, g_gmlp_v[l], w_gmlp[l], bs)
            ybc = _sconv(raw_c, w_sconv[l], CTX, BATCH * CTX)
            mc = _merge(yac, ybc, attn_c, gates_c, w_pa, w_pb, w_pc, l)
            c1, fc = _outproj(mc, w_o, cs, g_ffn[l], mods5, l, True)
            gc = _ffn_up(fc, w_up, w_ffn_conv, l, CTX, BATCH * CTX)
            cs = _ffn_down(gc, w_down, c1, mods5, l, True)

    return _final_norm(xs, g_final).reshape(BATCH, SEQ, D)
```

```python
import functools
import math

import jax
import jax.numpy as jnp
from jax.experimental import pallas as pl
from jax.experimental.pallas import tpu as pltpu

F32 = jnp.float32
BF16 = jnp.bfloat16

D = 2048
BATCH = 4
SEQ = 2048
CTX = 256
DEPTH = 2
GRID_W = 64
N_HEADS = 16
N_KV = 4
HD = 128
GROUP = N_HEADS // N_KV
CHUNK = 128
GW = D // 2
GG = 8
CW = D // 2
D_FF = 5504
EPS = 1e-6
ROPE_THETA = 10000.0
KV_DIM = N_KV * HD
IN_DIM = D + 2 * KV_DIM + 2 * GW + 3 * CW + 3 * D
OFF_K = D
OFF_U = D + 2 * KV_DIM
OFF_B = OFF_U + 2 * GW
OFF_G = OFF_B + 3 * CW
R_K = D
R_V = D + KV_DIM
R_B = D + 2 * KV_DIM
R_C = R_B + CW
R_H = R_C + CW
LANES = 128

VMEM_PHYS_V7X = 64 * 1024 * 1024
VMEM_CAP = VMEM_PHYS_V7X - 8 * 1024 * 1024


def _cp(sem, vmem_mb):
    return pltpu.CompilerParams(
        dimension_semantics=sem,
        vmem_limit_bytes=min(int(vmem_mb * 1024 * 1024), VMEM_CAP))


def _rms(xf, g):
    ms = jnp.mean(xf * xf, axis=-1, keepdims=True)
    return xf * jax.lax.rsqrt(ms + EPS) * g


def _sigmoid(x):
    return 0.5 * jnp.tanh(0.5 * x) + 0.5


def _gelu_tanh(x):
    c = math.sqrt(2.0 / math.pi)
    return 0.5 * x * (1.0 + jnp.tanh(c * (x + 0.044715 * (x * x * x))))


def _mods_kernel(c_ref, w_ref, b_ref, o_ref):
    c = c_ref[...]
    s = (c * _sigmoid(c)).astype(BF16)
    w = w_ref[...].astype(BF16)
    o_ref[...] = jnp.dot(s, w, preferred_element_type=F32) + b_ref[...]


def _mods(cvec, w_ada, b_ada):
    tn = 2048
    nj = 6 * D // tn
    return pl.pallas_call(
        _mods_kernel,
        out_shape=jax.ShapeDtypeStruct((DEPTH, 8, 6 * D), F32),
        grid=(DEPTH, nj),
        in_specs=[
            pl.BlockSpec((8, D), lambda l, j: (0, 0)),
            pl.BlockSpec((None, D, tn), lambda l, j: (l, 0, j)),
            pl.BlockSpec((None, 1, tn), lambda l, j: (l, 0, j)),
        ],
        out_specs=pl.BlockSpec((None, 8, tn), lambda l, j: (l, 0, j)),
        compiler_params=_cp(("arbitrary", "arbitrary"), 52),
        name="adaln_mods",
    )(cvec, w_ada, b_ada.reshape(DEPTH, 1, 6 * D))


def _mod_spec(layer, k, row_of, width=D, col_of=None):
    if col_of is None:
        return pl.BlockSpec((None, None, None, 1, width),
                            lambda *g: (layer, row_of(*g), k, 0, 0))
    return pl.BlockSpec((None, None, None, 1, width),
                        lambda *g: (layer, row_of(*g), k, 0, col_of(*g)))


def _row_of(ctx, tiles_per_batch, axis):
    if ctx:
        return lambda *g: BATCH
    return lambda *g: g[axis] // tiles_per_batch


def _normmod_kernel(x_ref, g_ref, sh_ref, sc_ref, o_ref):
    y = _rms(x_ref[...], g_ref[...])
    o_ref[...] = (y * (1.0 + sc_ref[...]) + sh_ref[...]).astype(o_ref.dtype)


def _norm_kernel(x_ref, g_ref, o_ref):
    o_ref[...] = _rms(x_ref[...], g_ref[...]).astype(o_ref.dtype)


def _normmod(x, g, mods5, layer, ctx):
    t = x.shape[0]
    tm = 1024
    row_of = _row_of(ctx, SEQ // tm, 0)
    return pl.pallas_call(
        _normmod_kernel,
        out_shape=jax.ShapeDtypeStruct((t, D), BF16),
        grid=(t // tm,),
        in_specs=[
            pl.BlockSpec((tm, D), lambda i: (i, 0)),
            pl.BlockSpec((1, D), lambda i: (0, 0)),
            _mod_spec(layer, 0, row_of),
            _mod_spec(layer, 1, row_of),
        ],
        out_specs=pl.BlockSpec((tm, D), lambda i: (i, 0)),
        compiler_params=_cp(("arbitrary",), 44),
        name="norm_mod",
    )(x, g.reshape(1, D), mods5, mods5)


def _final_norm(x, g):
    t = x.shape[0]
    tm = 1024
    return pl.pallas_call(
        _norm_kernel,
        out_shape=jax.ShapeDtypeStruct((t, D), F32),
        grid=(t // tm,),
        in_specs=[
            pl.BlockSpec((tm, D), lambda i: (i, 0)),
            pl.BlockSpec((1, D), lambda i: (0, 0)),
        ],
        out_specs=pl.BlockSpec((tm, D), lambda i: (i, 0)),
        compiler_params=_cp(("arbitrary",), 48),
        name="final_norm",
    )(x, g.reshape(1, D))


TN_IN = 1024
IN_MCHUNK = {"raw": 1024, "sigmoid": 1024, "gelu": 512}
_ACTS = {"raw": lambda a: a, "gelu": _gelu_tanh, "sigmoid": _sigmoid}


def _inproj_kernel(h_ref, w_ref, o_ref, wb_ref, *, act):
    @pl.when(pl.program_id(1) == 0)
    def _():
        wb_ref[...] = w_ref[...].astype(BF16)

    w = wb_ref[...]
    mc = min(IN_MCHUNK[act], h_ref.shape[0])
    for r0 in range(0, h_ref.shape[0], mc):
        acc = jnp.dot(h_ref[r0:r0 + mc, :], w, preferred_element_type=F32)
        o_ref[r0:r0 + mc, :] = _ACTS[act](acc).astype(o_ref.dtype)


def _inproj(h, w, layer, col_tiles, act):
    t = h.shape[0]
    tm = min(t, 2048)
    first, nj, skip_at, skip = col_tiles

    def wmap(j, i):
        return (layer, 0, first + j + skip * (j // skip_at))

    return pl.pallas_call(
        functools.partial(_inproj_kernel, act=act),
        out_shape=jax.ShapeDtypeStruct((t, nj * TN_IN), BF16),
        grid=(nj, t // tm),
        in_specs=[
            pl.BlockSpec((tm, D), lambda j, i: (i, 0)),
            pl.BlockSpec((None, D, TN_IN), wmap),
        ],
        out_specs=pl.BlockSpec((tm, TN_IN), lambda j, i: (i, j)),
        scratch_shapes=[pltpu.VMEM((D, TN_IN), BF16)],
        compiler_params=_cp(("arbitrary", "arbitrary"), 56),
        name="in_proj_" + act,
    )(h, w)


COLS_RAW = (0, 6, 3, (OFF_B - OFF_U) // TN_IN)
COLS_GELU = (OFF_U // TN_IN, (OFF_B - OFF_U) // TN_IN, 1 << 20, 0)
COLS_GATE = (OFF_G // TN_IN, 3 * D // TN_IN, 1 << 20, 0)
COLS_KV = (OFF_K // TN_IN, 1, 1 << 20, 0)


def _rope_tables():
    rows = SEQ // GRID_W
    row = jnp.repeat(jnp.arange(rows, dtype=F32), GRID_W)
    col = jnp.tile(jnp.arange(GRID_W, dtype=F32), rows)
    n_freq = HD // 4
    inv_freq = ROPE_THETA ** (-jnp.arange(n_freq, dtype=F32) / n_freq)
    ar = row[:, None] * inv_freq
    ac = col[:, None] * inv_freq
    cos = jnp.concatenate([jnp.cos(ar), jnp.cos(ar), jnp.cos(ac), jnp.cos(ac)], axis=-1)
    sin = jnp.concatenate([-jnp.sin(ar), jnp.sin(ar), -jnp.sin(ac), jnp.sin(ac)], axis=-1)
    return cos, sin


def _headnorm_kernel(x_ref, g_ref, *rest, n_heads, scale, rope):
    def first_half_mask(rows):
        lane = jax.lax.broadcasted_iota(jnp.int32, (rows, HD), 1)
        return (lane % (HD // 2)) < (HD // 4)

    def swap_halves(v, mask):
        return jnp.where(mask, pltpu.roll(v, HD - HD // 4, axis=1), pltpu.roll(v, HD // 4, axis=1))

    g = g_ref[...] * scale
    if rope:
        cos_ref, sin_ref, o_ref = rest
        g_partner = swap_halves(jnp.broadcast_to(g, (8, HD)), first_half_mask(8))[0:1, :]
        gcos = cos_ref[...] * g
        gsin = sin_ref[...] * g_partner
        packed_mask = first_half_mask(x_ref.shape[0] // 2)
    else:
        (o_ref,) = rest
    for h in range(n_heads):
        xb = x_ref[:, h * HD:(h + 1) * HD]
        x = xb.astype(F32)
        if rope:
            rstd = jax.lax.rsqrt(jnp.mean(x * x, axis=-1, keepdims=True) + EPS)
            packed = pltpu.bitcast(xb, jnp.uint32)
            partner = pltpu.bitcast(swap_halves(packed, packed_mask), BF16).astype(F32)
            y = (x * gcos + partner * gsin) * rstd
        else:
            y = _rms(x, g)
        o_ref[:, h * HD:(h + 1) * HD] = y.astype(o_ref.dtype)


def _headnorm(src, col_blk, n_heads, g, scale, rope_tabs):
    t = src.shape[0]
    tm = 1024
    w = n_heads * HD
    rope = rope_tabs is not None
    in_specs = [
        pl.BlockSpec((tm, w), lambda i: (i, col_blk)),
        pl.BlockSpec((1, HD), lambda i: (0, 0)),
    ]
    args = [src, g.reshape(1, HD)]
    if rope:
        nt = SEQ // tm
        in_specs += [pl.BlockSpec((tm, HD), lambda i: (i % nt, 0))] * 2
        args += list(rope_tabs)
    return pl.pallas_call(
        functools.partial(_headnorm_kernel, n_heads=n_heads, scale=scale, rope=rope),
        out_shape=jax.ShapeDtypeStruct((t, w), BF16),
        grid=(t // tm,),
        in_specs=in_specs,
        out_specs=pl.BlockSpec((tm, w), lambda i: (i, 0)),
        compiler_params=_cp(("arbitrary",), 44),
        name="head_norm",
    )(*args)


TQ_SUB = 512
KEY_TILE = 256


def _attn_kernel(q_ref, *refs, with_latent):
    if with_latent:
        kx_ref, vx_ref, kc_ref, vc_ref, o_ref, k_s, vt_s = refs
        k_s[0:SEQ, :] = kx_ref[...]
        k_s[SEQ:SEQ + CTX, :] = kc_ref[...]
        vt_s[:, 0:SEQ] = vx_ref[...].T
        vt_s[:, SEQ:SEQ + CTX] = vc_ref[...].T
    else:
        kc_ref, vc_ref, o_ref, k_s, vt_s = refs
        k_s[...] = kc_ref[...]
        vt_s[...] = vc_ref[...].T
    tiles = list(range(0, k_s.shape[0], KEY_TILE))
    tqs = min(TQ_SUB, q_ref.shape[0])
    units = [(r0, g0) for r0 in range(0, q_ref.shape[0], tqs) for g0 in range(0, GROUP, 2)]

    def queries(u):
        r0, g0 = units[u]
        return jnp.concatenate([q_ref[r0:r0 + tqs, g0 * HD:(g0 + 1) * HD],
                                q_ref[r0:r0 + tqs, (g0 + 1) * HD:(g0 + 2) * HD]], axis=0)

    def scores(q, t0):
        return jax.lax.dot_general(k_s[t0:t0 + KEY_TILE, :], q, (((1,), (1,)), ((), ())),
                                   preferred_element_type=F32)

    def colmax(sts):
        m = None
        for st in sts:
            mc = jnp.max(st, axis=0, keepdims=True)
            m = mc if m is None else jnp.maximum(m, mc)
        return m

    sts = [scores(queries(0), t0) for t0 in tiles]
    m = colmax(sts)
    for u, (r0, g0) in enumerate(units):
        q_next = queries(u + 1) if u + 1 < len(units) else None
        sts_next = []
        l = None
        ot = None
        for t0, st in zip(tiles, sts):
            p = jnp.exp2(st - m)
            lc = jnp.sum(p, axis=0, keepdims=True)
            oc = jnp.dot(vt_s[:, t0:t0 + KEY_TILE], p.astype(BF16),
                         preferred_element_type=F32)
            l = lc if l is None else l + lc
            ot = oc if ot is None else ot + oc
            if q_next is not None:
                sts_next.append(scores(q_next, t0))
        ot = ot * (1.0 / l)
        o_ref[r0:r0 + tqs, g0 * HD:(g0 + 1) * HD] = ot[:, :tqs].T.astype(o_ref.dtype)
        o_ref[r0:r0 + tqs, (g0 + 1) * HD:(g0 + 2) * HD] = ot[:, tqs:].T.astype(o_ref.dtype)
        if q_next is not None:
            sts = sts_next
            m = colmax(sts)


def _attention(q, lq, latent, context):
    with_latent = latent is not None
    s_len = (SEQ if with_latent else 0) + CTX
    tq = min(lq, 2048)
    nq = lq // tq
    in_specs = [pl.BlockSpec((tq, GROUP * HD), lambda b, h, i: (b * nq + i, h))]
    args = [q]
    for src, rows in ((latent, SEQ), (context, CTX)):
        if src is None:
            continue
        k_arr, v_arr, v_blk = src
        in_specs += [pl.BlockSpec((rows, HD), lambda b, h, i: (b, h)),
                     pl.BlockSpec((rows, HD), lambda b, h, i, v_blk=v_blk: (b, v_blk + h))]
        args += [k_arr, v_arr]
    return pl.pallas_call(
        functools.partial(_attn_kernel, with_latent=with_latent),
        out_shape=jax.ShapeDtypeStruct((BATCH * lq, D), BF16),
        grid=(BATCH, N_KV, nq),
        in_specs=in_specs,
        out_specs=pl.BlockSpec((tq, GROUP * HD), lambda b, h, i: (b * nq + i, h)),
        scratch_shapes=[pltpu.VMEM((s_len, HD), BF16), pltpu.VMEM((HD, s_len), BF16)],
        compiler_params=_cp(("arbitrary", "arbitrary", "arbitrary"), 48),
        name="attention",
    )(*args)


def _gmlp_kernel(u_ref, v_ref, gv_ref, ws_ref, bs_ref, o_ref, *, tm):
    v = _rms(v_ref[...].astype(F32), gv_ref[...]).astype(BF16)
    for g in range(GG):
        c0 = g * CHUNK
        w = ws_ref[g].astype(BF16)
        for n in range(tm // CHUNK):
            r0 = n * CHUNK
            mixed = jnp.dot(w, v[r0:r0 + CHUNK, c0:c0 + CHUNK],
                            preferred_element_type=F32) + bs_ref[g]
            u = u_ref[r0:r0 + CHUNK, c0:c0 + CHUNK].astype(F32)
            o_ref[r0:r0 + CHUNK, c0:c0 + CHUNK] = (u * mixed).astype(o_ref.dtype)


def _gmlp(zg, gv, ws, bs):
    t = zg.shape[0]
    tm = 1024
    return pl.pallas_call(
        functools.partial(_gmlp_kernel, tm=tm),
        out_shape=jax.ShapeDtypeStruct((t, GW), BF16),
        grid=(t // tm,),
        in_specs=[
            pl.BlockSpec((tm, GW), lambda i: (i, 0)),
            pl.BlockSpec((tm, GW), lambda i: (i, 1)),
            pl.BlockSpec((1, GW), lambda i: (0, 0)),
            pl.BlockSpec((GG, CHUNK, CHUNK), lambda i: (0, 0, 0)),
            pl.BlockSpec((GG, CHUNK, CHUNK), lambda i: (0, 0, 0)),
        ],
        out_specs=pl.BlockSpec((tm, GW), lambda i: (i, 0)),
        compiler_params=_cp(("arbitrary",), 32),
        name="gmlp",
    )(zg, zg, gv.reshape(1, GW), ws, bs)


def _dwconv3(x, w, seq_len):
    rows = x.shape[0]
    pos = jax.lax.broadcasted_iota(jnp.int32, x.shape, 0) % seq_len
    prev = jnp.where(pos == 0, 0.0, pltpu.roll(x, 1, axis=0))
    nxt = jnp.where(pos == seq_len - 1, 0.0, pltpu.roll(x, rows - 1, axis=0))
    return prev * w[0:1, :] + x * w[1:2, :] + nxt * w[2:3, :]


def _sconv_kernel(b_ref, c_ref, h_ref, w_ref, o_ref, *, seq_len):
    t = c_ref[...].astype(F32) * h_ref[...].astype(F32)
    y = b_ref[...].astype(F32) * _dwconv3(t, w_ref[...], seq_len)
    o_ref[...] = y.astype(o_ref.dtype)


def _sconv(raw, w, seq_len, rows):
    t = raw.shape[0]
    tc = 512
    return pl.pallas_call(
        functools.partial(_sconv_kernel, seq_len=seq_len),
        out_shape=jax.ShapeDtypeStruct((t, CW), BF16),
        grid=(t // rows, CW // tc),
        in_specs=[
            pl.BlockSpec((rows, tc), lambda i, j: (i, R_B // tc + j)),
            pl.BlockSpec((rows, tc), lambda i, j: (i, R_C // tc + j)),
            pl.BlockSpec((rows, tc), lambda i, j: (i, R_H // tc + j)),
            pl.BlockSpec((3, tc), lambda i, j: (0, j)),
        ],
        out_specs=pl.BlockSpec((rows, tc), lambda i, j: (i, j)),
        compiler_params=_cp(("arbitrary", "arbitrary"), 40),
        name="short_conv",
    )(raw, raw, raw, w)


def _merge_kernel(ya_ref, yb_ref, at_ref, wa_ref, wb_ref, wc_ref,
                  ga_ref, gb_ref, gc_ref, o_ref, wa_s, wb_s, wc_s):
    @pl.when(pl.program_id(1) == 0)
    def _():
        wa_s[...] = wa_ref[...].astype(BF16)
        wb_s[...] = wb_ref[...].astype(BF16)
        wc_s[...] = wc_ref[...].astype(BF16)

    a = jnp.dot(ya_ref[...], wa_s[...], preferred_element_type=F32)
    m = ga_ref[...].astype(F32) * a
    b = jnp.dot(yb_ref[...], wb_s[...], preferred_element_type=F32)
    m = m + gb_ref[...].astype(F32) * b
    c = jnp.dot(at_ref[...], wc_s[...], preferred_element_type=F32)
    m = m + gc_ref[...].astype(F32) * c
    o_ref[...] = m.astype(o_ref.dtype)


def _merge(ya, yb, attn, gates, w_pa, w_pb, w_pc, layer):
    t = ya.shape[0]
    tm = 1024
    tn = 512
    nb = D // tn
    return pl.pallas_call(
        _merge_kernel,
        out_shape=jax.ShapeDtypeStruct((t, D), BF16),
        grid=(nb, t // tm),
        in_specs=[
            pl.BlockSpec((tm, GW), lambda j, i: (i, 0)),
            pl.BlockSpec((tm, CW), lambda j, i: (i, 0)),
            pl.BlockSpec((tm, D), lambda j, i: (i, 0)),
            pl.BlockSpec((None, GW, tn), lambda j, i: (layer, 0, j)),
            pl.BlockSpec((None, CW, tn), lambda j, i: (layer, 0, j)),
            pl.BlockSpec((None, D, tn), lambda j, i: (layer, 0, j)),
            pl.BlockSpec((tm, tn), lambda j, i: (i, j)),
            pl.BlockSpec((tm, tn), lambda j, i: (i, nb + j)),
            pl.BlockSpec((tm, tn), lambda j, i: (i, 2 * nb + j)),
        ],
        out_specs=pl.BlockSpec((tm, tn), lambda j, i: (i, j)),
        scratch_shapes=[pltpu.VMEM((GW, tn), BF16), pltpu.VMEM((CW, tn), BF16),
                        pltpu.VMEM((D, tn), BF16)],
        compiler_params=_cp(("arbitrary", "arbitrary"), 56),
        name="merge",
    )(ya, yb, attn, w_pa, w_pb, w_pc, gates, gates, gates)


def _outproj_kernel(m_ref, w_ref, x_ref, gt_ref, g_ref, sh_ref, sc_ref, x1_ref, h_ref, w_s):
    @pl.when(pl.program_id(0) == 0)
    def _():
        w_s[...] = w_ref[...].astype(BF16)

    y = jnp.dot(m_ref[...], w_s[...], preferred_element_type=F32)
    x1 = x_ref[...] + gt_ref[...] * y
    x1_ref[...] = x1
    h_ref[...] = (_rms(x1, g_ref[...]) * (1.0 + sc_ref[...]) + sh_ref[...]).astype(h_ref.dtype)


def _outproj(m, w_o, x, g_ffn, mods5, layer, ctx):
    t = m.shape[0]
    tm = 512
    row_of = _row_of(ctx, SEQ // tm, 0)
    return pl.pallas_call(
        _outproj_kernel,
        out_shape=(jax.ShapeDtypeStruct((t, D), F32), jax.ShapeDtypeStruct((t, D), BF16)),
        grid=(t // tm,),
        in_specs=[
            pl.BlockSpec((tm, D), lambda i: (i, 0)),
            pl.BlockSpec((None, D, D), lambda i: (layer, 0, 0), pipeline_mode=pl.Buffered(1)),
            pl.BlockSpec((tm, D), lambda i: (i, 0)),
            _mod_spec(layer, 2, row_of),
            pl.BlockSpec((1, D), lambda i: (0, 0)),
            _mod_spec(layer, 3, row_of),
            _mod_spec(layer, 4, row_of),
        ],
        out_specs=(pl.BlockSpec((tm, D), lambda i: (i, 0)),
                   pl.BlockSpec((tm, D), lambda i: (i, 0))),
        scratch_shapes=[pltpu.VMEM((D, D), BF16)],
        compiler_params=_cp(("arbitrary",), 56),
        name="out_proj",
    )(m, w_o, x, mods5, g_ffn.reshape(1, D), mods5, mods5)


FF_BLOCKS = D_FF // LANES
FF_SUBW = 4 * LANES
FF_NSUB = 1
TN_FF = FF_NSUB * FF_SUBW
FF_NBLK = TN_FF // LANES
FF_MCHUNK = 512


def _chunked_dot(h_ref, w):
    parts = [jnp.dot(h_ref[r0:r0 + FF_MCHUNK, :], w, preferred_element_type=F32)
             for r0 in range(0, h_ref.shape[0], FF_MCHUNK)]
    return jnp.concatenate(parts, axis=0)


def _ffn_up_kernel(h_ref, *refs, seq_len):
    a_refs = refs[0:FF_NBLK]
    b_refs = refs[FF_NBLK:2 * FF_NBLK]
    ca_refs = refs[2 * FF_NBLK:3 * FF_NBLK]
    cb_refs = refs[3 * FF_NBLK:4 * FF_NBLK]
    o_ref = refs[4 * FF_NBLK]
    ws = []
    for s in range(FF_NSUB):
        k0 = s * (FF_SUBW // LANES)
        k1 = k0 + FF_SUBW // LANES
        ws.append((jnp.concatenate([r[...] for r in a_refs[k0:k1]], axis=1).astype(BF16),
                   jnp.concatenate([r[...] for r in b_refs[k0:k1]], axis=1).astype(BF16),
                   jnp.concatenate([r[...] for r in ca_refs[k0:k1]], axis=1),
                   jnp.concatenate([r[...] for r in cb_refs[k0:k1]], axis=1)))
    for s in range(FF_NSUB):
        wa, wb, ca, cb = ws[s]
        a = _dwconv3(_chunked_dot(h_ref, wa), ca, seq_len)
        b = _dwconv3(_chunked_dot(h_ref, wb), cb, seq_len)
        o_ref[:, s * FF_SUBW:(s + 1) * FF_SUBW] = (a * _sigmoid(a) * b).astype(o_ref.dtype)


def _ffn_up(h, w_up, w_conv, layer, seq_len, rows):
    t = h.shape[0]
    nj = pl.cdiv(D_FF, TN_FF)
    last = 2 * FF_BLOCKS - 1

    def wspec(rows_, off, k):
        return pl.BlockSpec((None, rows_, LANES),
                            lambda i, j: (layer, 0, jnp.minimum(off + FF_NBLK * j + k, last)))

    blocks = range(FF_NBLK)
    specs = ([wspec(D, 0, k) for k in blocks] + [wspec(D, FF_BLOCKS, k) for k in blocks]
             + [wspec(3, 0, k) for k in blocks] + [wspec(3, FF_BLOCKS, k) for k in blocks])
    return pl.pallas_call(
        functools.partial(_ffn_up_kernel, seq_len=seq_len),
        out_shape=jax.ShapeDtypeStruct((t, D_FF), BF16),
        grid=(t // rows, nj),
        in_specs=[pl.BlockSpec((rows, D), lambda i, j: (i, 0))] + specs,
        out_specs=pl.BlockSpec((rows, TN_FF), lambda i, j: (i, j)),
        compiler_params=_cp(("arbitrary", "arbitrary"), 56),
        name="ffn_up",
    )(h, *([w_up] * (2 * FF_NBLK)), *([w_conv] * (2 * FF_NBLK)))


def _ffn_down_kernel(g_ref, w_ref, x_ref, gt_ref, o_ref, w_s):
    @pl.when(pl.program_id(1) == 0)
    def _():
        w_s[...] = w_ref[...].astype(BF16)

    y = jnp.dot(g_ref[...], w_s[...], preferred_element_type=F32)
    o_ref[...] = x_ref[...] + gt_ref[...] * y


def _ffn_down(g, w_down, x, mods5, layer, ctx):
    t = g.shape[0]
    tm = t if ctx else 512
    tn = 512
    row_of = _row_of(ctx, SEQ // tm, 1)
    lhs_mode = {"pipeline_mode": pl.Buffered(1)} if ctx else {}
    return pl.pallas_call(
        _ffn_down_kernel,
        out_shape=jax.ShapeDtypeStruct((t, D), F32),
        grid=(D // tn, t // tm),
        in_specs=[
            pl.BlockSpec((tm, D_FF), lambda j, i: (i, 0), **lhs_mode),
            pl.BlockSpec((None, D_FF, tn), lambda j, i: (layer, 0, j)),
            pl.BlockSpec((tm, tn), lambda j, i: (i, j)),
            _mod_spec(layer, 5, row_of, tn, lambda j, i: j),
        ],
        out_specs=pl.BlockSpec((tm, tn), lambda j, i: (i, j)),
        scratch_shapes=[pltpu.VMEM((D_FF, tn), BF16)],
        compiler_params=_cp(("arbitrary", "arbitrary"), 54),
        name="ffn_down",
    )(g, w_down, x, mods5)


def kernel(x, c, ctx, c_ctx, w_ada, b_ada, g_mix, w_in, g_q, g_k, w_gmlp, b_gmlp, g_gmlp_v,
           w_sconv, w_pa, w_pb, w_pc, w_o, g_ffn, w_up, w_ffn_conv, w_down, g_final):
    cvec = jnp.concatenate([c, c_ctx[None, :], jnp.zeros((3, D), F32)], axis=0)
    mods5 = _mods(cvec, w_ada, b_ada).reshape(DEPTH, 8, 6, 1, D)
    rope_tabs = _rope_tables()
    q_scale = HD ** -0.5 * math.log2(math.e)

    xs = x.reshape(BATCH * SEQ, D)
    cs = ctx.reshape(BATCH * CTX, D)

    for l in range(DEPTH):
        last = l == DEPTH - 1
        bs = jnp.broadcast_to(b_gmlp[l][:, :, None], (GG, CHUNK, CHUNK))
        hx = _normmod(xs, g_mix[l], mods5, l, False)
        hc = _normmod(cs, g_mix[l], mods5, l, True)

        raw_x = _inproj(hx, w_in, l, COLS_RAW, "raw")
        zg_x = _inproj(hx, w_in, l, COLS_GELU, "gelu")
        gates_x = _inproj(hx, w_in, l, COLS_GATE, "sigmoid")
        if not last:
            raw_c = _inproj(hc, w_in, l, COLS_RAW, "raw")
            kc = _headnorm(raw_c, R_K // KV_DIM, N_KV, g_k[l], 1.0, None)
            ctx_kv = (kc, raw_c, R_V // HD)
        else:
            kv_c = _inproj(hc, w_in, l, COLS_KV, "raw")
            kc = _headnorm(kv_c, 0, N_KV, g_k[l], 1.0, None)
            ctx_kv = (kc, kv_c, KV_DIM // HD)
        qx = _headnorm(raw_x, 0, N_HEADS, g_q[l], q_scale, rope_tabs)
        kx = _headnorm(raw_x, R_K // KV_DIM, N_KV, g_k[l], 1.0, rope_tabs)
        attn_x = _attention(qx, SEQ, (kx, raw_x, R_V // HD), ctx_kv)
        ya = _gmlp(zg_x, g_gmlp_v[l], w_gmlp[l], bs)
        yb = _sconv(raw_x, w_sconv[l], SEQ, SEQ)
        m = _merge(ya, yb, attn_x, gates_x, w_pa, w_pb, w_pc, l)
        x1, fx = _outproj(m, w_o, xs, g_ffn[l], mods5, l, False)

        gx = _ffn_up(fx, w_up, w_ffn_conv, l, SEQ, SEQ)
        xs = _ffn_down(gx, w_down, x1, mods5, l, False)

        if not last:
            zg_c = _inproj(hc, w_in, l, COLS_GELU, "gelu")
            gates_c = _inproj(hc, w_in, l, COLS_GATE, "sigmoid")
            qc = _headnorm(raw_c, 0, N_HEADS, g_q[l], q_scale, None)
            attn_c = _attention(qc, CTX, None, ctx_kv)
            yac = _gmlp(zg_c, g_gmlp_v[l], w_gmlp[l], bs)
            ybc = _sconv(raw_c, w_sconv[l], CTX, BATCH * CTX)
            mc = _merge(yac, ybc, attn_c, gates_c, w_pa, w_pb, w_pc, l)
            c1, fc = _outproj(mc, w_o, cs, g_ffn[l], mods5, l, True)
            gc = _ffn_up(fc, w_up, w_ffn_conv, l, CTX, BATCH * CTX)
            cs = _ffn_down(gc, w_down, c1, mods5, l, True)

    return _final_norm(xs, g_final).reshape(BATCH, SEQ, D)
```

```python
import functools
import math

import jax
import jax.numpy as jnp
from jax.experimental import pallas as pl
from jax.experimental.pallas import tpu as pltpu

F32 = jnp.float32
BF16 = jnp.bfloat16

D = 2048
BATCH = 4
SEQ = 2048
CTX = 256
DEPTH = 2
GRID_W = 64
N_HEADS = 16
N_KV = 4
HD = 128
GROUP = N_HEADS // N_KV
CHUNK = 128
GW = D // 2
GG = 8
CW = D // 2
D_FF = 5504
EPS = 1e-6
ROPE_THETA = 10000.0
KV_DIM = N_KV * HD
IN_DIM = D + 2 * KV_DIM + 2 * GW + 3 * CW + 3 * D
OFF_K = D
OFF_U = D + 2 * KV_DIM
OFF_B = OFF_U + 2 * GW
OFF_G = OFF_B + 3 * CW
R_K = D
R_V = D + KV_DIM
R_B = D + 2 * KV_DIM
R_C = R_B + CW
R_H = R_C + CW
LANES = 128

VMEM_PHYS_V7X = 64 * 1024 * 1024
VMEM_CAP = VMEM_PHYS_V7X - 8 * 1024 * 1024


def _cp(sem, vmem_mb):
    return pltpu.CompilerParams(
        dimension_semantics=sem,
        vmem_limit_bytes=min(int(vmem_mb * 1024 * 1024), VMEM_CAP))


def _rms(xf, g):
    ms = jnp.mean(xf * xf, axis=-1, keepdims=True)
    return xf * jax.lax.rsqrt(ms + EPS) * g


def _sigmoid(x):
    return 0.5 * jnp.tanh(0.5 * x) + 0.5


def _gelu_tanh(x):
    c = math.sqrt(2.0 / math.pi)
    return 0.5 * x * (1.0 + jnp.tanh(c * (x + 0.044715 * (x * x * x))))


def _mods_kernel(c_ref, w_ref, b_ref, o_ref):
    c = c_ref[...]
    s = (c * _sigmoid(c)).astype(BF16)
    w = w_ref[...].astype(BF16)
    o_ref[...] = jnp.dot(s, w, preferred_element_type=F32) + b_ref[...]


def _mods(cvec, w_ada, b_ada):
    tn = 2048
    nj = 6 * D // tn
    return pl.pallas_call(
        _mods_kernel,
        out_shape=jax.ShapeDtypeStruct((DEPTH, 8, 6 * D), F32),
        grid=(DEPTH, nj),
        in_specs=[
            pl.BlockSpec((8, D), lambda l, j: (0, 0)),
            pl.BlockSpec((None, D, tn), lambda l, j: (l, 0, j)),
            pl.BlockSpec((None, 1, tn), lambda l, j: (l, 0, j)),
        ],
        out_specs=pl.BlockSpec((None, 8, tn), lambda l, j: (l, 0, j)),
        compiler_params=_cp(("arbitrary", "arbitrary"), 52),
        name="adaln_mods",
    )(cvec, w_ada, b_ada.reshape(DEPTH, 1, 6 * D))


def _mod_spec(layer, k, row_of, width=D, col_of=None):
    if col_of is None:
        return pl.BlockSpec((None, None, None, 1, width),
                            lambda *g: (layer, row_of(*g), k, 0, 0))
    return pl.BlockSpec((None, None, None, 1, width),
                        lambda *g: (layer, row_of(*g), k, 0, col_of(*g)))


def _row_of(ctx, tiles_per_batch, axis):
    if ctx:
        return lambda *g: BATCH
    return lambda *g: g[axis] // tiles_per_batch


def _normmod_kernel(x_ref, g_ref, sh_ref, sc_ref, o_ref):
    y = _rms(x_ref[...], g_ref[...])
    o_ref[...] = (y * (1.0 + sc_ref[...]) + sh_ref[...]).astype(o_ref.dtype)


def _norm_kernel(x_ref, g_ref, o_ref):
    o_ref[...] = _rms(x_ref[...], g_ref[...]).astype(o_ref.dtype)


def _normmod(x, g, mods5, layer, ctx):
    t = x.shape[0]
    tm = 1024
    row_of = _row_of(ctx, SEQ // tm, 0)
    return pl.pallas_call(
        _normmod_kernel,
        out_shape=jax.ShapeDtypeStruct((t, D), BF16),
        grid=(t // tm,),
        in_specs=[
            pl.BlockSpec((tm, D), lambda i: (i, 0)),
            pl.BlockSpec((1, D), lambda i: (0, 0)),
            _mod_spec(layer, 0, row_of),
            _mod_spec(layer, 1, row_of),
        ],
        out_specs=pl.BlockSpec((tm, D), lambda i: (i, 0)),
        compiler_params=_cp(("arbitrary",), 44),
        name="norm_mod",
    )(x, g.reshape(1, D), mods5, mods5)


def _final_norm(x, g):
    t = x.shape[0]
    tm = 1024
    return pl.pallas_call(
        _norm_kernel,
        out_shape=jax.ShapeDtypeStruct((t, D), F32),
        grid=(t // tm,),
        in_specs=[
            pl.BlockSpec((tm, D), lambda i: (i, 0)),
            pl.BlockSpec((1, D), lambda i: (0, 0)),
        ],
        out_specs=pl.BlockSpec((tm, D), lambda i: (i, 0)),
        compiler_params=_cp(("arbitrary",), 48),
        name="final_norm",
    )(x, g.reshape(1, D))


TN_IN = 1024
IN_MCHUNK = {"raw": 1024, "sigmoid": 1024, "gelu": 512}
_ACTS = {"raw": lambda a: a, "gelu": _gelu_tanh, "sigmoid": _sigmoid}


def _inproj_kernel(h_ref, w_ref, o_ref, wb_ref, *, act):
    @pl.when(pl.program_id(1) == 0)
    def _():
        wb_ref[...] = w_ref[...].astype(BF16)

    w = wb_ref[...]
    mc = min(IN_MCHUNK[act], h_ref.shape[0])
    for r0 in range(0, h_ref.shape[0], mc):
        acc = jnp.dot(h_ref[r0:r0 + mc, :], w, preferred_element_type=F32)
        o_ref[r0:r0 + mc, :] = _ACTS[act](acc).astype(o_ref.dtype)


def _inproj(h, w, layer, col_tiles, act):
    t = h.shape[0]
    tm = min(t, 2048)
    first, nj, skip_at, skip = col_tiles

    def wmap(j, i):
        return (layer, 0, first + j + skip * (j // skip_at))

    return pl.pallas_call(
        functools.partial(_inproj_kernel, act=act),
        out_shape=jax.ShapeDtypeStruct((t, nj * TN_IN), BF16),
        grid=(nj, t // tm),
        in_specs=[
            pl.BlockSpec((tm, D), lambda j, i: (i, 0)),
            pl.BlockSpec((None, D, TN_IN), wmap),
        ],
        out_specs=pl.BlockSpec((tm, TN_IN), lambda j, i: (i, j)),
        scratch_shapes=[pltpu.VMEM((D, TN_IN), BF16)],
        compiler_params=_cp(("arbitrary", "arbitrary"), 56),
        name="in_proj_" + act,
    )(h, w)


COLS_RAW = (0, 6, 3, (OFF_B - OFF_U) // TN_IN)
COLS_GELU = (OFF_U // TN_IN, (OFF_B - OFF_U) // TN_IN, 1 << 20, 0)
COLS_GATE = (OFF_G // TN_IN, 3 * D // TN_IN, 1 << 20, 0)
COLS_KV = (OFF_K // TN_IN, 1, 1 << 20, 0)


def _rope_tables():
    rows = SEQ // GRID_W
    row = jnp.repeat(jnp.arange(rows, dtype=F32), GRID_W)
    col = jnp.tile(jnp.arange(GRID_W, dtype=F32), rows)
    n_freq = HD // 4
    inv_freq = ROPE_THETA ** (-jnp.arange(n_freq, dtype=F32) / n_freq)
    ar = row[:, None] * inv_freq
    ac = col[:, None] * inv_freq
    cos = jnp.concatenate([jnp.cos(ar), jnp.cos(ar), jnp.cos(ac), jnp.cos(ac)], axis=-1)
    sin = jnp.concatenate([-jnp.sin(ar), jnp.sin(ar), -jnp.sin(ac), jnp.sin(ac)], axis=-1)
    return cos, sin


def _headnorm_qk_kernel(xq_ref, xk_ref, gq_ref, gk_ref, *rest, q_scale, rope):
    tabs, (oq_ref, ok_ref) = rest[:-2], rest[-2:]
    _headnorm_kernel(xq_ref, gq_ref, *tabs, oq_ref, n_heads=N_HEADS, scale=q_scale, rope=rope)
    _headnorm_kernel(xk_ref, gk_ref, *tabs, ok_ref, n_heads=N_KV, scale=1.0, rope=rope)


def _headnorm_qk(src, g_q, g_k, q_scale, rope_tabs):
    t = src.shape[0]
    tm = 1024
    rope = rope_tabs is not None
    in_specs = [
        pl.BlockSpec((tm, D), lambda i: (i, 0)),
        pl.BlockSpec((tm, KV_DIM), lambda i: (i, D // KV_DIM)),
        pl.BlockSpec((1, HD), lambda i: (0, 0)),
        pl.BlockSpec((1, HD), lambda i: (0, 0)),
    ]
    args = [src, src, g_q.reshape(1, HD), g_k.reshape(1, HD)]
    if rope:
        nt = SEQ // tm
        in_specs += [pl.BlockSpec((tm, HD), lambda i: (i % nt, 0))] * 2
        args += list(rope_tabs)
    return pl.pallas_call(
        functools.partial(_headnorm_qk_kernel, q_scale=q_scale, rope=rope),
        out_shape=(jax.ShapeDtypeStruct((t, D), BF16), jax.ShapeDtypeStruct((t, KV_DIM), BF16)),
        grid=(t // tm,),
        in_specs=in_specs,
        out_specs=(pl.BlockSpec((tm, D), lambda i: (i, 0)),
                   pl.BlockSpec((tm, KV_DIM), lambda i: (i, 0))),
        compiler_params=_cp(("arbitrary",), 48),
        name="head_norm_qk",
    )(*args)


def _headnorm_kernel(x_ref, g_ref, *rest, n_heads, scale, rope):
    def first_half_mask(rows):
        lane = jax.lax.broadcasted_iota(jnp.int32, (rows, HD), 1)
        return (lane % (HD // 2)) < (HD // 4)

    def swap_halves(v, mask):
        return jnp.where(mask, pltpu.roll(v, HD - HD // 4, axis=1), pltpu.roll(v, HD // 4, axis=1))

    g = g_ref[...] * scale
    if rope:
        cos_ref, sin_ref, o_ref = rest
        g_partner = swap_halves(jnp.broadcast_to(g, (8, HD)), first_half_mask(8))[0:1, :]
        gcos = cos_ref[...] * g
        gsin = sin_ref[...] * g_partner
        packed_mask = first_half_mask(x_ref.shape[0] // 2)
    else:
        (o_ref,) = rest
    for h in range(n_heads):
        xb = x_ref[:, h * HD:(h + 1) * HD]
        x = xb.astype(F32)
        if rope:
            rstd = jax.lax.rsqrt(jnp.mean(x * x, axis=-1, keepdims=True) + EPS)
            packed = pltpu.bitcast(xb, jnp.uint32)
            partner = pltpu.bitcast(swap_halves(packed, packed_mask), BF16).astype(F32)
            y = (x * gcos + partner * gsin) * rstd
        else:
            y = _rms(x, g)
        o_ref[:, h * HD:(h + 1) * HD] = y.astype(o_ref.dtype)


def _headnorm(src, col_blk, n_heads, g, scale, rope_tabs):
    t = src.shape[0]
    tm = 1024
    w = n_heads * HD
    rope = rope_tabs is not None
    in_specs = [
        pl.BlockSpec((tm, w), lambda i: (i, col_blk)),
        pl.BlockSpec((1, HD), lambda i: (0, 0)),
    ]
    args = [src, g.reshape(1, HD)]
    if rope:
        nt = SEQ // tm
        in_specs += [pl.BlockSpec((tm, HD), lambda i: (i % nt, 0))] * 2
        args += list(rope_tabs)
    return pl.pallas_call(
        functools.partial(_headnorm_kernel, n_heads=n_heads, scale=scale, rope=rope),
        out_shape=jax.ShapeDtypeStruct((t, w), BF16),
        grid=(t // tm,),
        in_specs=in_specs,
        out_specs=pl.BlockSpec((tm, w), lambda i: (i, 0)),
        compiler_params=_cp(("arbitrary",), 44),
        name="head_norm",
    )(*args)


TQ_SUB = 512
KEY_TILE = 256


def _attn_kernel(q_ref, *refs, with_latent):
    if with_latent:
        kx_ref, vx_ref, kc_ref, vc_ref, o_ref, k_s, vt_s = refs
        k_s[0:SEQ, :] = kx_ref[...]
        k_s[SEQ:SEQ + CTX, :] = kc_ref[...]
        vt_s[:, 0:SEQ] = vx_ref[...].T
        vt_s[:, SEQ:SEQ + CTX] = vc_ref[...].T
    else:
        kc_ref, vc_ref, o_ref, k_s, vt_s = refs
        k_s[...] = kc_ref[...]
        vt_s[...] = vc_ref[...].T
    tiles = list(range(0, k_s.shape[0], KEY_TILE))
    tqs = min(TQ_SUB, q_ref.shape[0])
    units = [(r0, g0) for r0 in range(0, q_ref.shape[0], tqs) for g0 in range(0, GROUP, 2)]

    def queries(u):
        r0, g0 = units[u]
        return jnp.concatenate([q_ref[r0:r0 + tqs, g0 * HD:(g0 + 1) * HD],
                                q_ref[r0:r0 + tqs, (g0 + 1) * HD:(g0 + 2) * HD]], axis=0)

    def scores(q, t0):
        return jax.lax.dot_general(k_s[t0:t0 + KEY_TILE, :], q, (((1,), (1,)), ((), ())),
                                   preferred_element_type=F32)

    def colmax(sts):
        m = None
        for st in sts:
            mc = jnp.max(st, axis=0, keepdims=True)
            m = mc if m is None else jnp.maximum(m, mc)
        return m

    sts = [scores(queries(0), t0) for t0 in tiles]
    m = colmax(sts)
    for u, (r0, g0) in enumerate(units):
        q_next = queries(u + 1) if u + 1 < len(units) else None
        sts_next = []
        l = None
        ot = None
        for t0, st in zip(tiles, sts):
            p = jnp.exp2(st - m)
            lc = jnp.sum(p, axis=0, keepdims=True)
            oc = jnp.dot(vt_s[:, t0:t0 + KEY_TILE], p.astype(BF16),
                         preferred_element_type=F32)
            l = lc if l is None else l + lc
            ot = oc if ot is None else ot + oc
            if q_next is not None:
                sts_next.append(scores(q_next, t0))
        ot = ot * (1.0 / l)
        o_ref[r0:r0 + tqs, g0 * HD:(g0 + 1) * HD] = ot[:, :tqs].T.astype(o_ref.dtype)
        o_ref[r0:r0 + tqs, (g0 + 1) * HD:(g0 + 2) * HD] = ot[:, tqs:].T.astype(o_ref.dtype)
        if q_next is not None:
            sts = sts_next
            m = colmax(sts)


def _attention(q, lq, latent, context):
    with_latent = latent is not None
    s_len = (SEQ if with_latent else 0) + CTX
    tq = min(lq, 2048)
    nq = lq // tq
    in_specs = [pl.BlockSpec((tq, GROUP * HD), lambda b, h, i: (b * nq + i, h))]
    args = [q]
    for src, rows in ((latent, SEQ), (context, CTX)):
        if src is None:
            continue
        k_arr, v_arr, v_blk = src
        in_specs += [pl.BlockSpec((rows, HD), lambda b, h, i: (b, h)),
                     pl.BlockSpec((rows, HD), lambda b, h, i, v_blk=v_blk: (b, v_blk + h))]
        args += [k_arr, v_arr]
    return pl.pallas_call(
        functools.partial(_attn_kernel, with_latent=with_latent),
        out_shape=jax.ShapeDtypeStruct((BATCH * lq, D), BF16),
        grid=(BATCH, N_KV, nq),
        in_specs=in_specs,
        out_specs=pl.BlockSpec((tq, GROUP * HD), lambda b, h, i: (b * nq + i, h)),
        scratch_shapes=[pltpu.VMEM((s_len, HD), BF16), pltpu.VMEM((HD, s_len), BF16)],
        compiler_params=_cp(("arbitrary", "arbitrary", "arbitrary"), 48),
        name="attention",
    )(*args)


def _gmlp_kernel(u_ref, v_ref, gv_ref, ws_ref, bs_ref, o_ref, *, tm):
    v = _rms(v_ref[...].astype(F32), gv_ref[...]).astype(BF16)
    for g in range(GG):
        c0 = g * CHUNK
        w = ws_ref[g].astype(BF16)
        for n in range(tm // CHUNK):
            r0 = n * CHUNK
            mixed = jnp.dot(w, v[r0:r0 + CHUNK, c0:c0 + CHUNK],
                            preferred_element_type=F32) + bs_ref[g]
            u = u_ref[r0:r0 + CHUNK, c0:c0 + CHUNK].astype(F32)
            o_ref[r0:r0 + CHUNK, c0:c0 + CHUNK] = (u * mixed).astype(o_ref.dtype)


def _gmlp(zg, gv, ws, bs):
    t = zg.shape[0]
    tm = 1024
    return pl.pallas_call(
        functools.partial(_gmlp_kernel, tm=tm),
        out_shape=jax.ShapeDtypeStruct((t, GW), BF16),
        grid=(t // tm,),
        in_specs=[
            pl.BlockSpec((tm, GW), lambda i: (i, 0)),
            pl.BlockSpec((tm, GW), lambda i: (i, 1)),
            pl.BlockSpec((1, GW), lambda i: (0, 0)),
            pl.BlockSpec((GG, CHUNK, CHUNK), lambda i: (0, 0, 0)),
            pl.BlockSpec((GG, CHUNK, CHUNK), lambda i: (0, 0, 0)),
        ],
        out_specs=pl.BlockSpec((tm, GW), lambda i: (i, 0)),
        compiler_params=_cp(("arbitrary",), 32),
        name="gmlp",
    )(zg, zg, gv.reshape(1, GW), ws, bs)


def _dwconv3(x, w, seq_len):
    rows = x.shape[0]
    pos = jax.lax.broadcasted_iota(jnp.int32, x.shape, 0) % seq_len
    prev = jnp.where(pos == 0, 0.0, pltpu.roll(x, 1, axis=0))
    nxt = jnp.where(pos == seq_len - 1, 0.0, pltpu.roll(x, rows - 1, axis=0))
    return prev * w[0:1, :] + x * w[1:2, :] + nxt * w[2:3, :]


def _sconv_kernel(b_ref, c_ref, h_ref, w_ref, o_ref, *, seq_len):
    t = c_ref[...].astype(F32) * h_ref[...].astype(F32)
    y = b_ref[...].astype(F32) * _dwconv3(t, w_ref[...], seq_len)
    o_ref[...] = y.astype(o_ref.dtype)


def _sconv(raw, w, seq_len, rows):
    t = raw.shape[0]
    tc = 512
    return pl.pallas_call(
        functools.partial(_sconv_kernel, seq_len=seq_len),
        out_shape=jax.ShapeDtypeStruct((t, CW), BF16),
        grid=(t // rows, CW // tc),
        in_specs=[
            pl.BlockSpec((rows, tc), lambda i, j: (i, R_B // tc + j)),
            pl.BlockSpec((rows, tc), lambda i, j: (i, R_C // tc + j)),
            pl.BlockSpec((rows, tc), lambda i, j: (i, R_H // tc + j)),
            pl.BlockSpec((3, tc), lambda i, j: (0, j)),
        ],
        out_specs=pl.BlockSpec((rows, tc), lambda i, j: (i, j)),
        compiler_params=_cp(("arbitrary", "arbitrary"), 40),
        name="short_conv",
    )(raw, raw, raw, w)


def _merge_kernel(ya_ref, yb_ref, at_ref, wa_ref, wb_ref, wc_ref,
                  ga_ref, gb_ref, gc_ref, o_ref, wa_s, wb_s, wc_s):
    @pl.when(pl.program_id(1) == 0)
    def _():
        wa_s[...] = wa_ref[...].astype(BF16)
        wb_s[...] = wb_ref[...].astype(BF16)
        wc_s[...] = wc_ref[...].astype(BF16)

    a = jnp.dot(ya_ref[...], wa_s[...], preferred_element_type=F32)
    m = ga_ref[...].astype(F32) * a
    b = jnp.dot(yb_ref[...], wb_s[...], preferred_element_type=F32)
    m = m + gb_ref[...].astype(F32) * b
    c = jnp.dot(at_ref[...], wc_s[...], preferred_element_type=F32)
    m = m + gc_ref[...].astype(F32) * c
    o_ref[...] = m.astype(o_ref.dtype)


def _merge(ya, yb, attn, gates, w_pa, w_pb, w_pc, layer):
    t = ya.shape[0]
    tm = 1024
    tn = 512
    nb = D // tn
    return pl.pallas_call(
        _merge_kernel,
        out_shape=jax.ShapeDtypeStruct((t, D), BF16),
        grid=(nb, t // tm),
        in_specs=[
            pl.BlockSpec((tm, GW), lambda j, i: (i, 0)),
            pl.BlockSpec((tm, CW), lambda j, i: (i, 0)),
            pl.BlockSpec((tm, D), lambda j, i: (i, 0)),
            pl.BlockSpec((None, GW, tn), lambda j, i: (layer, 0, j)),
            pl.BlockSpec((None, CW, tn), lambda j, i: (layer, 0, j)),
            pl.BlockSpec((None, D, tn), lambda j, i: (layer, 0, j)),
            pl.BlockSpec((tm, tn), lambda j, i: (i, j)),
            pl.BlockSpec((tm, tn), lambda j, i: (i, nb + j)),
            pl.BlockSpec((tm, tn), lambda j, i: (i, 2 * nb + j)),
        ],
        out_specs=pl.BlockSpec((tm, tn), lambda j, i: (i, j)),
        scratch_shapes=[pltpu.VMEM((GW, tn), BF16), pltpu.VMEM((CW, tn), BF16),
                        pltpu.VMEM((D, tn), BF16)],
        compiler_params=_cp(("arbitrary", "arbitrary"), 56),
        name="merge",
    )(ya, yb, attn, w_pa, w_pb, w_pc, gates, gates, gates)


def _outproj_kernel(m_ref, w_ref, x_ref, gt_ref, g_ref, sh_ref, sc_ref, x1_ref, h_ref, w_s):
    @pl.when(pl.program_id(0) == 0)
    def _():
        w_s[...] = w_ref[...].astype(BF16)

    y = jnp.dot(m_ref[...], w_s[...], preferred_element_type=F32)
    x1 = x_ref[...] + gt_ref[...] * y
    x1_ref[...] = x1
    h_ref[...] = (_rms(x1, g_ref[...]) * (1.0 + sc_ref[...]) + sh_ref[...]).astype(h_ref.dtype)


def _outproj(m, w_o, x, g_ffn, mods5, layer, ctx):
    t = m.shape[0]
    tm = 512
    row_of = _row_of(ctx, SEQ // tm, 0)
    return pl.pallas_call(
        _outproj_kernel,
        out_shape=(jax.ShapeDtypeStruct((t, D), F32), jax.ShapeDtypeStruct((t, D), BF16)),
        grid=(t // tm,),
        in_specs=[
            pl.BlockSpec((tm, D), lambda i: (i, 0)),
            pl.BlockSpec((None, D, D), lambda i: (layer, 0, 0), pipeline_mode=pl.Buffered(1)),
            pl.BlockSpec((tm, D), lambda i: (i, 0)),
            _mod_spec(layer, 2, row_of),
            pl.BlockSpec((1, D), lambda i: (0, 0)),
            _mod_spec(layer, 3, row_of),
            _mod_spec(layer, 4, row_of),
        ],
        out_specs=(pl.BlockSpec((tm, D), lambda i: (i, 0)),
                   pl.BlockSpec((tm, D), lambda i: (i, 0))),
        scratch_shapes=[pltpu.VMEM((D, D), BF16)],
        compiler_params=_cp(("arbitrary",), 56),
        name="out_proj",
    )(m, w_o, x, mods5, g_ffn.reshape(1, D), mods5, mods5)


FF_BLOCKS = D_FF // LANES
FF_SUBW = 4 * LANES
FF_NSUB = 1
TN_FF = FF_NSUB * FF_SUBW
FF_NBLK = TN_FF // LANES
FF_MCHUNK = 512


def _chunked_dot(h_ref, w):
    parts = [jnp.dot(h_ref[r0:r0 + FF_MCHUNK, :], w, preferred_element_type=F32)
             for r0 in range(0, h_ref.shape[0], FF_MCHUNK)]
    return jnp.concatenate(parts, axis=0)


def _ffn_up_kernel(h_ref, *refs, seq_len):
    a_refs = refs[0:FF_NBLK]
    b_refs = refs[FF_NBLK:2 * FF_NBLK]
    ca_refs = refs[2 * FF_NBLK:3 * FF_NBLK]
    cb_refs = refs[3 * FF_NBLK:4 * FF_NBLK]
    o_ref = refs[4 * FF_NBLK]
    ws = []
    for s in range(FF_NSUB):
        k0 = s * (FF_SUBW // LANES)
        k1 = k0 + FF_SUBW // LANES
        ws.append((jnp.concatenate([r[...] for r in a_refs[k0:k1]], axis=1).astype(BF16),
                   jnp.concatenate([r[...] for r in b_refs[k0:k1]], axis=1).astype(BF16),
                   jnp.concatenate([r[...] for r in ca_refs[k0:k1]], axis=1),
                   jnp.concatenate([r[...] for r in cb_refs[k0:k1]], axis=1)))
    for s in range(FF_NSUB):
        wa, wb, ca, cb = ws[s]
        a = _dwconv3(_chunked_dot(h_ref, wa), ca, seq_len)
        b = _dwconv3(_chunked_dot(h_ref, wb), cb, seq_len)
        o_ref[:, s * FF_SUBW:(s + 1) * FF_SUBW] = (a * _sigmoid(a) * b).astype(o_ref.dtype)


def _ffn_up(h, w_up, w_conv, layer, seq_len, rows):
    t = h.shape[0]
    nj = pl.cdiv(D_FF, TN_FF)
    last = 2 * FF_BLOCKS - 1

    def wspec(rows_, off, k):
        return pl.BlockSpec((None, rows_, LANES),
                            lambda i, j: (layer, 0, jnp.minimum(off + FF_NBLK * j + k, last)))

    blocks = range(FF_NBLK)
    specs = ([wspec(D, 0, k) for k in blocks] + [wspec(D, FF_BLOCKS, k) for k in blocks]
             + [wspec(3, 0, k) for k in blocks] + [wspec(3, FF_BLOCKS, k) for k in blocks])
    return pl.pallas_call(
        functools.partial(_ffn_up_kernel, seq_len=seq_len),
        out_shape=jax.ShapeDtypeStruct((t, D_FF), BF16),
        grid=(t // rows, nj),
        in_specs=[pl.BlockSpec((rows, D), lambda i, j: (i, 0))] + specs,
        out_specs=pl.BlockSpec((rows, TN_FF), lambda i, j: (i, j)),
        compiler_params=_cp(("arbitrary", "arbitrary"), 56),
        name="ffn_up",
    )(h, *([w_up] * (2 * FF_NBLK)), *([w_conv] * (2 * FF_NBLK)))


def _ffn_down_kernel(g_ref, w_ref, x_ref, gt_ref, o_ref, w_s):
    @pl.when(pl.program_id(1) == 0)
    def _():
        w_s[...] = w_ref[...].astype(BF16)

    y = jnp.dot(g_ref[...], w_s[...], preferred_element_type=F32)
    o_ref[...] = x_ref[...] + gt_ref[...] * y


def _ffn_down(g, w_down, x, mods5, layer, ctx):
    t = g.shape[0]
    tm = t if ctx else 512
    tn = 512
    row_of = _row_of(ctx, SEQ // tm, 1)
    lhs_mode = {"pipeline_mode": pl.Buffered(1)} if ctx else {}
    return pl.pallas_call(
        _ffn_down_kernel,
        out_shape=jax.ShapeDtypeStruct((t, D), F32),
        grid=(D // tn, t // tm),
        in_specs=[
            pl.BlockSpec((tm, D_FF), lambda j, i: (i, 0), **lhs_mode),
            pl.BlockSpec((None, D_FF, tn), lambda j, i: (layer, 0, j)),
            pl.BlockSpec((tm, tn), lambda j, i: (i, j)),
            _mod_spec(layer, 5, row_of, tn, lambda j, i: j),
        ],
        out_specs=pl.BlockSpec((tm, tn), lambda j, i: (i, j)),
        scratch_shapes=[pltpu.VMEM((D_FF, tn), BF16)],
        compiler_params=_cp(("arbitrary", "arbitrary"), 54),
        name="ffn_down",
    )(g, w_down, x, mods5)


def kernel(x, c, ctx, c_ctx, w_ada, b_ada, g_mix, w_in, g_q, g_k, w_gmlp, b_gmlp, g_gmlp_v,
           w_sconv, w_pa, w_pb, w_pc, w_o, g_ffn, w_up, w_ffn_conv, w_down, g_final):
    cvec = jnp.concatenate([c, c_ctx[None, :], jnp.zeros((3, D), F32)], axis=0)
    mods5 = _mods(cvec, w_ada, b_ada).reshape(DEPTH, 8, 6, 1, D)
    rope_tabs = _rope_tables()
    q_scale = HD ** -0.5 * math.log2(math.e)

    xs = x.reshape(BATCH * SEQ, D)
    cs = ctx.reshape(BATCH * CTX, D)

    for l in range(DEPTH):
        last = l == DEPTH - 1
        bs = jnp.broadcast_to(b_gmlp[l][:, :, None], (GG, CHUNK, CHUNK))
        hx = _normmod(xs, g_mix[l], mods5, l, False)
        hc = _normmod(cs, g_mix[l], mods5, l, True)

        raw_x = _inproj(hx, w_in, l, COLS_RAW, "raw")
        zg_x = _inproj(hx, w_in, l, COLS_GELU, "gelu")
        gates_x = _inproj(hx, w_in, l, COLS_GATE, "sigmoid")
        if not last:
            raw_c = _inproj(hc, w_in, l, COLS_RAW, "raw")
            qc, kc = _headnorm_qk(raw_c, g_q[l], g_k[l], q_scale, None)
            ctx_kv = (kc, raw_c, R_V // HD)
        else:
            kv_c = _inproj(hc, w_in, l, COLS_KV, "raw")
            kc = _headnorm(kv_c, 0, N_KV, g_k[l], 1.0, None)
            ctx_kv = (kc, kv_c, KV_DIM // HD)
        qx, kx = _headnorm_qk(raw_x, g_q[l], g_k[l], q_scale, rope_tabs)
        attn_x = _attention(qx, SEQ, (kx, raw_x, R_V // HD), ctx_kv)
        ya = _gmlp(zg_x, g_gmlp_v[l], w_gmlp[l], bs)
        yb = _sconv(raw_x, w_sconv[l], SEQ, SEQ)
        m = _merge(ya, yb, attn_x, gates_x, w_pa, w_pb, w_pc, l)
        x1, fx = _outproj(m, w_o, xs, g_ffn[l], mods5, l, False)

        gx = _ffn_up(fx, w_up, w_ffn_conv, l, SEQ, SEQ)
        xs = _ffn_down(gx, w_down, x1, mods5, l, False)

        if not last:
            zg_c = _inproj(hc, w_in, l, COLS_GELU, "gelu")
            gates_c = _inproj(hc, w_in, l, COLS_GATE, "sigmoid")
            attn_c = _attention(qc, CTX, None, ctx_kv)
            yac = _gmlp(zg_c, g_gmlp_v[l], w_gmlp[l], bs)
            ybc = _sconv(raw_c, w_sconv[l], CTX, BATCH * CTX)
            mc = _merge(yac, ybc, attn_c, gates_c, w_pa, w_pb, w_pc, l)
            c1, fc = _outproj(mc, w_o, cs, g_ffn[l], mods5, l, True)
            gc = _ffn_up(fc, w_up, w_ffn_conv, l, CTX, BATCH * CTX)
            cs = _ffn_down(gc, w_down, c1, mods5, l, True)

    return _final_norm(xs, g_final).reshape(BATCH, SEQ, D)
```
